```python
import math
import jax, jax.numpy as jnp
from jax import lax
import numpy as np

D_MODEL = 2048
BATCH = 4
SEQ = 2048
DEPTH = 1

ATTN_WIDTH = D_MODEL // 2
REC_WIDTH = D_MODEL // 2
N_ATTN_HEADS = 8
ATTN_HEAD_DIM = ATTN_WIDTH // N_ATTN_HEADS
QK_DIM = ATTN_HEAD_DIM // 2
N_REC_BLOCKS = 8
REC_BLOCK = REC_WIDTH // N_REC_BLOCKS
CONV_WIDTH = 4
LRU_C = 8.0
ROPE_THETA = 10000.0
Q_BLOCK = 128
NORM_EPS = 1e-6
IN_WIDTH = 4 * ATTN_WIDTH + 2 * REC_WIDTH
SPLITS = [ATTN_WIDTH, 2 * ATTN_WIDTH, 3 * ATTN_WIDTH, 4 * ATTN_WIDTH, 4 * ATTN_WIDTH + REC_WIDTH]

kernel_name = "hymba_diffattn_rglru_hybrid"


def rms_norm(x, g):
    xf = x.astype(jnp.float32)
    y = xf * lax.rsqrt(jnp.mean(xf * xf, axis=-1, keepdims=True) + NORM_EPS)
    return (y * g.astype(jnp.float32)).astype(x.dtype)


def rope(t, cos, sin):
    tf = t.astype(jnp.float32)
    t1, t2 = jnp.split(tf, 2, axis=-1)
    out = jnp.concatenate([t1 * cos - t2 * sin, t2 * cos + t1 * sin], axis=-1)
    return out.astype(t.dtype)


def diff_attention(q, k, v, lam):
    seq = q.shape[1]
    q = q.transpose(0, 2, 3, 1, 4)
    k = k.transpose(0, 2, 3, 1, 4)
    v = v.transpose(0, 2, 1, 3)
    scale = QK_DIM ** -0.5
    outs = []
    for s0 in range(0, seq, Q_BLOCK):
        end = s0 + Q_BLOCK
        qb = q[:, :, :, s0:end]
        kb = k[:, :, :, :end]
        vb = v[:, :, :end]
        s = jnp.einsum('bhmqd,bhmkd->bhmqk', qb, kb).astype(jnp.float32) * scale
        mask = jnp.arange(end)[None, :] <= (s0 + jnp.arange(Q_BLOCK))[:, None]
        s = jnp.where(mask, s, -jnp.inf)
        p = jax.nn.softmax(s, axis=-1)
        w = p[:, :, 0] - lam * p[:, :, 1]
        outs.append(jnp.einsum('bhqk,bhkd->bhqd', w.astype(vb.dtype), vb))
    o = jnp.concatenate(outs, axis=2)
    return o.transpose(0, 2, 1, 3)


def causal_conv(x, w, b):
    c = x.shape[-1]
    y = lax.conv_general_dilated(
        x, w[:, None, :].astype(x.dtype), window_strides=(1,),
        padding=[(CONV_WIDTH - 1, 0)], dimension_numbers=('NWC', 'WIO', 'NWC'),
        feature_group_count=c)
    return y + b.astype(x.dtype)


def rg_lru(x, w_a, b_a, w_x, b_x, lru_lambda):
    bsz, seq, c = x.shape
    xb = x.reshape(bsz, seq, N_REC_BLOCKS, REC_BLOCK)
    r = jax.nn.sigmoid(jnp.einsum('bsni,nij->bsnj', xb, w_a).reshape(bsz, seq, c).astype(jnp.float32)
                       + b_a.astype(jnp.float32))
    i = jax.nn.sigmoid(jnp.einsum('bsni,nij->bsnj', xb, w_x).reshape(bsz, seq, c).astype(jnp.float32)
                       + b_x.astype(jnp.float32))
    log_a = -LRU_C * r * jax.nn.softplus(-lru_lambda.astype(jnp.float32))
    a = jnp.exp(log_a)
    mult = jnp.sqrt(-jnp.expm1(2.0 * log_a))
    u = mult * (i * x.astype(jnp.float32))

    def combine(left, right):
        a_l, u_l = left
        a_r, u_r = right
        return a_l * a_r, a_r * u_l + u_r

    _, h = lax.associative_scan(combine, (a, u), axis=1)
    return h.astype(x.dtype)


def setup_inputs(seed: int = 0) -> dict:
    key = jax.random.key(seed)
    ks = jax.random.split(key, 20)
    f32 = jnp.float32
    x = jax.random.normal(ks[0], (BATCH, SEQ, D_MODEL), f32)
    positions = (jnp.arange(SEQ, dtype=jnp.int32)[None, :]
                 + jax.random.randint(ks[1], (BATCH, 1), 0, 1024, dtype=jnp.int32))
    norm_gain = 1.0 + 0.02 * jax.random.normal(ks[2], (DEPTH, D_MODEL), f32)
    w_in = jax.random.normal(ks[3], (DEPTH, D_MODEL, IN_WIDTH), f32) * D_MODEL ** -0.5
    lambda_q1 = 0.1 * jax.random.normal(ks[4], (DEPTH, QK_DIM), f32)
    lambda_k1 = 0.1 * jax.random.normal(ks[5], (DEPTH, QK_DIM), f32)
    lambda_q2 = 0.1 * jax.random.normal(ks[6], (DEPTH, QK_DIM), f32)
    lambda_k2 = 0.1 * jax.random.normal(ks[7], (DEPTH, QK_DIM), f32)
    subln_gain = 1.0 + 0.02 * jax.random.normal(ks[8], (DEPTH, ATTN_HEAD_DIM), f32)
    conv_w = jax.random.normal(ks[9], (DEPTH, CONV_WIDTH, REC_WIDTH), f32) * CONV_WIDTH ** -0.5
    conv_b = 0.01 * jax.random.normal(ks[10], (DEPTH, REC_WIDTH), f32)
    w_a = jax.random.normal(ks[11], (DEPTH, N_REC_BLOCKS, REC_BLOCK, REC_BLOCK), f32) * REC_BLOCK ** -0.5
    b_a = 0.01 * jax.random.normal(ks[12], (DEPTH, REC_WIDTH), f32)
    w_x = jax.random.normal(ks[13], (DEPTH, N_REC_BLOCKS, REC_BLOCK, REC_BLOCK), f32) * REC_BLOCK ** -0.5
    b_x = 0.01 * jax.random.normal(ks[14], (DEPTH, REC_WIDTH), f32)
    a_c = jax.random.uniform(ks[15], (DEPTH, REC_WIDTH), f32, 0.9, 0.999)
    a_base = a_c ** (1.0 / LRU_C)
    lru_lambda = jnp.log(a_base) - jnp.log1p(-a_base)
    w_out = jax.random.normal(ks[16], (DEPTH, ATTN_WIDTH + REC_WIDTH, D_MODEL), f32) * (ATTN_WIDTH + REC_WIDTH) ** -0.5
    final_gain = 1.0 + 0.02 * jax.random.normal(ks[17], (D_MODEL,), f32)
    return {"x": x, "positions": positions, "norm_gain": norm_gain, "w_in": w_in,
            "lambda_q1": lambda_q1, "lambda_k1": lambda_k1, "lambda_q2": lambda_q2,
            "lambda_k2": lambda_k2, "subln_gain": subln_gain, "conv_w": conv_w,
            "conv_b": conv_b, "w_a": w_a, "b_a": b_a, "w_x": w_x, "b_x": b_x,
            "lru_lambda": lru_lambda, "w_out": w_out, "final_gain": final_gain}


def reference(x, positions, norm_gain, w_in, lambda_q1, lambda_k1, lambda_q2, lambda_k2,
              subln_gain, conv_w, conv_b, w_a, b_a, w_x, b_x, lru_lambda, w_out, final_gain):
    bsz, seq, _ = x.shape
    inv_freq = ROPE_THETA ** (-jnp.arange(0, QK_DIM, 2, dtype=jnp.float32) / QK_DIM)
    ang = positions.astype(jnp.float32)[..., None] * inv_freq
    cos = jnp.cos(ang)[:, :, None, None, :]
    sin = jnp.sin(ang)[:, :, None, None, :]
    for l in range(DEPTH):
        h = rms_norm(x, norm_gain[l])
        proj = h @ w_in[l]
        q, k, v, g_attn, xr, g_rec = jnp.split(proj, SPLITS, axis=-1)
        q = rope(q.reshape(bsz, seq, N_ATTN_HEADS, 2, QK_DIM), cos, sin)
        k = rope(k.reshape(bsz, seq, N_ATTN_HEADS, 2, QK_DIM), cos, sin)
        v = v.reshape(bsz, seq, N_ATTN_HEADS, ATTN_HEAD_DIM)
        lam_init = 0.8 - 0.6 * math.exp(-0.3 * l)
        lam = (jnp.exp(jnp.sum(lambda_q1[l].astype(jnp.float32) * lambda_k1[l].astype(jnp.float32)))
               - jnp.exp(jnp.sum(lambda_q2[l].astype(jnp.float32) * lambda_k2[l].astype(jnp.float32)))
               + lam_init)
        o = diff_attention(q, k, v, lam)
        o = rms_norm(o, subln_gain[l]) * (1.0 - lam_init)
        o = o.reshape(bsz, seq, ATTN_WIDTH) * jax.nn.silu(g_attn)
        r = causal_conv(xr, conv_w[l], conv_b[l])
        r = rg_lru(r, w_a[l], b_a[l], w_x[l], b_x[l], lru_lambda[l])
        r = r * jax.nn.silu(g_rec)
        mix = jnp.concatenate([o, r], axis=-1)
        x = x + mix @ w_out[l]
    return rms_norm(x, final_gain)
```

```python
import functools
import math

import jax
import jax.numpy as jnp
from jax import lax
from jax.experimental import pallas as pl
from jax.experimental.pallas import tpu as pltpu

F32 = jnp.float32
BF16 = jnp.bfloat16

D_MODEL = 2048
ATTN_WIDTH = D_MODEL // 2
REC_WIDTH = D_MODEL // 2
N_ATTN_HEADS = 8
HEAD_DIM = ATTN_WIDTH // N_ATTN_HEADS
QK_DIM = HEAD_DIM // 2
ROT_HALF = QK_DIM // 2
N_REC_BLOCKS = 8
REC_BLOCK = REC_WIDTH // N_REC_BLOCKS
CONV_WIDTH = 4
LRU_C = 8.0
ROPE_THETA = 10000.0
NORM_EPS = 1e-6
IN_WIDTH = 4 * ATTN_WIDTH + 2 * REC_WIDTH
LAM_INIT = 0.8 - 0.6 * math.exp(-0.3 * 0)
QK_SCALE = QK_DIM ** -0.5

SUBLANES = 8
LANES = 128

TM_IN = 512
TN_IN = 1024
TQ = 256
TK = 256
TS = 256
TM_OUT = 512
VMEM_LIMIT = 48 * 1024 * 1024


def _rope_partner(blk, lane):
    fwd = pltpu.roll(blk, LANES - ROT_HALF, axis=1)
    bwd = pltpu.roll(blk, ROT_HALF, axis=1)
    return jnp.where((lane % QK_DIM) < ROT_HALF, fwd, bwd)


def _inproj_kernel(x_ref, pos_ref, invf_ref, gain_ref, w_ref, o_ref,
                   h_scr, cos_scr, sin_scr):
    j = pl.program_id(1)
    tm = x_ref.shape[0]

    @pl.when(j == 0)
    def _():
        x = x_ref[...]
        ms = jnp.mean(x * x, axis=-1, keepdims=True)
        h = x * lax.rsqrt(ms + NORM_EPS) * gain_ref[...]
        h_scr[...] = h.astype(BF16)
        ang = pos_ref[...].astype(F32) * invf_ref[...]
        lane = lax.broadcasted_iota(jnp.int32, ang.shape, 1)
        sign = jnp.where((lane % QK_DIM) < ROT_HALF, -1.0, 1.0).astype(F32)
        cos_scr[...] = jnp.cos(ang)
        sin_scr[...] = jnp.sin(ang) * sign

    @pl.when(j >= 2)
    def _():
        o_ref[...] = jnp.dot(h_scr[...], w_ref[...],
                             preferred_element_type=F32).astype(BF16)

    @pl.when(j < 2)
    def _():
        scale = jnp.where(j == 0, QK_SCALE, 1.0).astype(F32)
        cosf = cos_scr[...] * scale
        sinf = sin_scr[...] * scale
        lane = lax.broadcasted_iota(jnp.int32, (tm, LANES), 1)
        for pair in range(TN_IN // (2 * LANES)):
            c0 = pair * 2 * LANES
            acc = jnp.dot(h_scr[...], w_ref[:, c0:c0 + 2 * LANES],
                          preferred_element_type=F32)
            for half in range(2):
                blk = acc[:, half * LANES:(half + 1) * LANES]
                rot = blk * cosf + _rope_partner(blk, lane) * sinf
                o_ref[:, c0 + half * LANES:c0 + (half + 1) * LANES] = rot.astype(BF16)


def _inproj(x2, pos2, invf, gain, w_in_bf16):
    m = x2.shape[0]
    grid = (m // TM_IN, IN_WIDTH // TN_IN)
    return pl.pallas_call(
        _inproj_kernel,
        grid=grid,
        in_specs=[
            pl.BlockSpec((TM_IN, D_MODEL), lambda i, j: (i, 0)),
            pl.BlockSpec((TM_IN, 1), lambda i, j: (i, 0)),
            pl.BlockSpec((1, LANES), lambda i, j: (0, 0)),
            pl.BlockSpec((1, D_MODEL), lambda i, j: (0, 0)),
            pl.BlockSpec((D_MODEL, TN_IN), lambda i, j: (0, j)),
        ],
        out_specs=pl.BlockSpec((TM_IN, TN_IN), lambda i, j: (i, j)),
        out_shape=jax.ShapeDtypeStruct((m, IN_WIDTH), BF16),
        scratch_shapes=[
            pltpu.VMEM((TM_IN, D_MODEL), BF16),
            pltpu.VMEM((TM_IN, LANES), F32),
            pltpu.VMEM((TM_IN, LANES), F32),
        ],
        compiler_params=pltpu.CompilerParams(
            dimension_semantics=("arbitrary", "arbitrary"),
            vmem_limit_bytes=VMEM_LIMIT),
        name="inproj_rope",
    )(x2, pos2, invf, gain, w_in_bf16)


def _lane_fold(x, op):
    out = x[:, :LANES]
    for c in range(1, x.shape[1] // LANES):
        out = op(out, x[:, c * LANES:(c + 1) * LANES])
    return out


def _attn_kernel(q_ref, k_ref, v_ref, g_ref, lq1_ref, lk1_ref, lq2_ref, lk2_ref,
                 sg_ref, o_ref, s_scr):
    seq = q_ref.shape[0]
    n_q = seq // TQ
    lam = (jnp.exp(jnp.sum(lq1_ref[...] * lk1_ref[...], axis=-1, keepdims=True))
           - jnp.exp(jnp.sum(lq2_ref[...] * lk2_ref[...], axis=-1, keepdims=True))
           + LAM_INIT)
    lane = lax.broadcasted_iota(jnp.int32, (TQ, LANES), 1)
    row = lax.broadcasted_iota(jnp.int32, (2 * TQ, TK), 0)
    col = lax.broadcasted_iota(jnp.int32, (2 * TQ, TK), 1)
    causal = col <= jnp.where(row >= TQ, row - TQ, row)

    def q_tile(i, carry):
        r0 = pl.multiple_of(i * TQ, TQ)
        q = q_ref[pl.ds(r0, TQ), :]
        zero = jnp.zeros_like(q)
        qcat = jnp.concatenate([jnp.where(lane < QK_DIM, q, zero),
                                jnp.where(lane >= QK_DIM, q, zero)], axis=0)

        def scores(jc):
            kj = k_ref[pl.ds(pl.multiple_of(jc * TK, TK), TK), :]
            return lax.dot_general(qcat, kj, (((1,), (1,)), ((), ())),
                                   preferred_element_type=F32)

        def pass_scores(jc, m):
            s = scores(jc)
            s_scr[jc] = s
            return jnp.maximum(m, _lane_fold(s, jnp.maximum))

        m = lax.fori_loop(0, i, pass_scores,
                          jnp.full((2 * TQ, LANES), -jnp.inf, F32))
        s = jnp.where(causal, scores(i), -jnp.inf)
        s_scr[i] = s
        m = jnp.maximum(m, _lane_fold(s, jnp.maximum))
        m = jnp.max(m, axis=-1, keepdims=True)

        def pass_exp(jc, l):
            p = jnp.exp(s_scr[jc] - m)
            s_scr[jc] = p
            return l + _lane_fold(p, jnp.add)

        l = lax.fori_loop(0, i + 1, pass_exp, jnp.zeros((2 * TQ, LANES), F32))
        l = jnp.sum(l, axis=-1, keepdims=True)
        inv = 1.0 / l
        c1 = inv[:TQ]
        c2 = inv[TQ:] * lam

        def pass_pv(jc, acc):
            p = s_scr[jc]
            w = p[:TQ] * c1 - p[TQ:] * c2
            vj = v_ref[pl.ds(pl.multiple_of(jc * TK, TK), TK), :]
            return acc + jnp.dot(w.astype(BF16), vj, preferred_element_type=F32)

        o = lax.fori_loop(0, i + 1, pass_pv, jnp.zeros((TQ, HEAD_DIM), F32))
        ms = jnp.mean(o * o, axis=-1, keepdims=True)
        o = o * lax.rsqrt(ms + NORM_EPS) * sg_ref[...] * (1.0 - LAM_INIT)
        g = g_ref[pl.ds(r0, TQ), :].astype(F32)
        o_ref[pl.ds(r0, TQ), :] = (o * (g * jax.nn.sigmoid(g))).astype(BF16)
        return carry

    lax.fori_loop(0, n_q, q_tile, 0)


def _attention(proj, lq1, lk1, lq2, lk2, subln, bsz, seq):
    n_k = seq // TK
    head_block = (seq, HEAD_DIM)
    q_group, k_group, v_group, g_group = (g * N_ATTN_HEADS for g in range(4))
    small = lambda w: pl.BlockSpec((1, w), lambda b, h: (0, 0))
    return pl.pallas_call(
        _attn_kernel,
        grid=(bsz, N_ATTN_HEADS),
        in_specs=[
            pl.BlockSpec(head_block, lambda b, h: (b, q_group + h)),
            pl.BlockSpec(head_block, lambda b, h: (b, k_group + h)),
            pl.BlockSpec(head_block, lambda b, h: (b, v_group + h)),
            pl.BlockSpec(head_block, lambda b, h: (b, g_group + h)),
            small(QK_DIM), small(QK_DIM), small(QK_DIM), small(QK_DIM),
            small(HEAD_DIM),
        ],
        out_specs=pl.BlockSpec(head_block, lambda b, h: (b, h)),
        out_shape=jax.ShapeDtypeStruct((bsz * seq, ATTN_WIDTH), BF16),
        scratch_shapes=[pltpu.VMEM((n_k, 2 * TQ, TK), F32)],
        compiler_params=pltpu.CompilerParams(
            dimension_semantics=("arbitrary", "arbitrary"),
            vmem_limit_bytes=VMEM_LIMIT),
        name="diff_attention",
    )(proj, proj, proj, proj, lq1, lk1, lq2, lk2, subln)


def _rglru_kernel(xr_ref, gr_ref, cw_ref, cb_ref, wg_ref, ba_ref, bx_ref, lam_ref,
                  o_ref, xe_scr, a_scr, u_scr, h_scr, hc_scr):
    t = pl.program_id(1)
    halo = SUBLANES

    @pl.when(t == 0)
    def _():
        xe_scr[0:halo, :] = jnp.zeros((halo, REC_WIDTH), F32)
        hc_scr[...] = jnp.zeros_like(hc_scr)

    @pl.when(t > 0)
    def _():
        xe_scr[0:halo, :] = xe_scr[TS:TS + halo, :]

    xe_scr[halo:halo + TS, :] = xr_ref[...].astype(F32)

    groups = TS // SUBLANES
    sub = lax.broadcasted_iota(jnp.int32, (groups, SUBLANES, REC_BLOCK), 1)
    for n in range(N_REC_BLOCKS):
        cs = slice(n * REC_BLOCK, (n + 1) * REC_BLOCK)
        y = cb_ref[:, cs]
        for tap in range(CONV_WIDTH):
            shift = CONV_WIDTH - 1 - tap
            y = y + cw_ref[tap:tap + 1, cs] * xe_scr[halo - shift:halo - shift + TS, cs]
        gates = jnp.dot(y.astype(BF16), wg_ref[n], preferred_element_type=F32)
        r = jax.nn.sigmoid(gates[:, :REC_BLOCK] + ba_ref[:, cs])
        ig = jax.nn.sigmoid(gates[:, REC_BLOCK:] + bx_ref[:, cs])
        log_a = r * (-LRU_C * jax.nn.softplus(-lam_ref[:, cs]))
        a = jnp.exp(log_a)
        mult = jnp.sqrt(-jnp.tanh(log_a) * (a * a + 1.0))
        u = mult * (ig * y)
        a3 = a.reshape(groups, SUBLANES, REC_BLOCK)
        u3 = u.reshape(groups, SUBLANES, REC_BLOCK)
        d = 1
        while d < SUBLANES:
            keep = sub >= d
            a_prev = pltpu.roll(a3, d, axis=1)
            u_prev = pltpu.roll(u3, d, axis=1)
            u3 = jnp.where(keep, a3 * u_prev, 0.0) + u3
            a3 = jnp.where(keep, a3 * a_prev, a3)
            d *= 2
        a_scr[:, cs] = a3.reshape(TS, REC_BLOCK)
        u_scr[:, cs] = u3.reshape(TS, REC_BLOCK)

    h_prev = hc_scr[...]
    for g in range(groups):
        rows = slice(g * SUBLANES, (g + 1) * SUBLANES)
        hg = u_scr[rows, :] + a_scr[rows, :] * h_prev
        h_scr[rows, :] = hg
        h_prev = jnp.broadcast_to(hg[SUBLANES - 1:SUBLANES, :], hg.shape)
    hc_scr[...] = h_prev

    gr = gr_ref[...].astype(F32)
    o_ref[...] = (h_scr[...] * (gr * jax.nn.sigmoid(gr))).astype(BF16)


def _rglru(proj, conv_w, conv_b, w_gate, b_a, b_x, lru_lambda, bsz, seq):
    n_t = seq // TS
    xr_group, gr_group = 4, 5
    full = lambda shape: pl.BlockSpec(shape, lambda b, t: (0,) * len(shape))
    return pl.pallas_call(
        _rglru_kernel,
        grid=(bsz, n_t),
        in_specs=[
            pl.BlockSpec((TS, REC_WIDTH), lambda b, t: (b * n_t + t, xr_group)),
            pl.BlockSpec((TS, REC_WIDTH), lambda b, t: (b * n_t + t, gr_group)),
            full((CONV_WIDTH, REC_WIDTH)),
            full((1, REC_WIDTH)),
            full((N_REC_BLOCKS, REC_BLOCK, 2 * REC_BLOCK)),
            full((1, REC_WIDTH)), full((1, REC_WIDTH)), full((1, REC_WIDTH)),
        ],
        out_specs=pl.BlockSpec((TS, REC_WIDTH), lambda b, t: (b * n_t + t, 0)),
        out_shape=jax.ShapeDtypeStruct((bsz * seq, REC_WIDTH), BF16),
        scratch_shapes=[
            pltpu.VMEM((TS + SUBLANES, REC_WIDTH), F32),
            pltpu.VMEM((TS, REC_WIDTH), F32),
            pltpu.VMEM((TS, REC_WIDTH), F32),
            pltpu.VMEM((TS, REC_WIDTH), F32),
            pltpu.VMEM((SUBLANES, REC_WIDTH), F32),
        ],
        compiler_params=pltpu.CompilerParams(
            dimension_semantics=("arbitrary", "arbitrary"),
            vmem_limit_bytes=VMEM_LIMIT),
        name="conv_rglru",
    )(proj, proj, conv_w, conv_b, w_gate, b_a, b_x, lru_lambda)


def _outproj_kernel(x_ref, o_ref, r_ref, w_ref, gain_ref, y_ref):
    y = x_ref[...]
    y = y + jnp.dot(o_ref[...], w_ref[0:ATTN_WIDTH, :], preferred_element_type=F32)
    y = y + jnp.dot(r_ref[...], w_ref[ATTN_WIDTH:, :], preferred_element_type=F32)
    ms = jnp.mean(y * y, axis=-1, keepdims=True)
    y_ref[...] = y * lax.rsqrt(ms + NORM_EPS) * gain_ref[...]


def _outproj(x2, o, r, w_out_bf16, final_gain):
    m = x2.shape[0]
    return pl.pallas_call(
        _outproj_kernel,
        grid=(m // TM_OUT,),
        in_specs=[
            pl.BlockSpec((TM_OUT, D_MODEL), lambda i: (i, 0)),
            pl.BlockSpec((TM_OUT, ATTN_WIDTH), lambda i: (i, 0)),
            pl.BlockSpec((TM_OUT, REC_WIDTH), lambda i: (i, 0)),
            pl.BlockSpec((ATTN_WIDTH + REC_WIDTH, D_MODEL), lambda i: (0, 0)),
            pl.BlockSpec((1, D_MODEL), lambda i: (0, 0)),
        ],
        out_specs=pl.BlockSpec((TM_OUT, D_MODEL), lambda i: (i, 0)),
        out_shape=jax.ShapeDtypeStruct((m, D_MODEL), F32),
        compiler_params=pltpu.CompilerParams(
            dimension_semantics=("arbitrary",),
            vmem_limit_bytes=VMEM_LIMIT),
        name="outproj_norm",
    )(x2, o, r, w_out_bf16, final_gain)


def kernel(x, positions, norm_gain, w_in, lambda_q1, lambda_k1, lambda_q2, lambda_k2,
           subln_gain, conv_w, conv_b, w_a, b_a, w_x, b_x, lru_lambda, w_out, final_gain):
    bsz, seq, _ = x.shape
    assert x.shape[2] == D_MODEL and norm_gain.shape[0] == 1
    assert seq % max(TQ, TK, TS) == 0 and (bsz * seq) % max(TM_IN, TM_OUT) == 0
    x2 = x.reshape(bsz * seq, D_MODEL)
    pos2 = positions.reshape(bsz * seq, 1)
    inv_freq = ROPE_THETA ** (-jnp.arange(0, QK_DIM, 2, dtype=F32) / QK_DIM)
    invf = jnp.tile(inv_freq, LANES // ROT_HALF).reshape(1, LANES)

    proj = _inproj(x2, pos2, invf, norm_gain[0].reshape(1, D_MODEL),
                   w_in[0].astype(BF16))
    o = _attention(proj, lambda_q1[0].reshape(1, QK_DIM), lambda_k1[0].reshape(1, QK_DIM),
                   lambda_q2[0].reshape(1, QK_DIM), lambda_k2[0].reshape(1, QK_DIM),
                   subln_gain[0].reshape(1, HEAD_DIM), bsz, seq)
    w_gate = jnp.concatenate([w_a[0], w_x[0]], axis=-1).astype(BF16)
    r = _rglru(proj, conv_w[0], conv_b[0].reshape(1, REC_WIDTH), w_gate,
               b_a[0].reshape(1, REC_WIDTH), b_x[0].reshape(1, REC_WIDTH),
               lru_lambda[0].reshape(1, REC_WIDTH), bsz, seq)
    y = _outproj(x2, o, r, w_out[0].astype(BF16), final_gain.reshape(1, D_MODEL))
    return y.reshape(bsz, seq, D_MODEL)
```

```python
import functools
import math

import jax
import jax.numpy as jnp
from jax import lax
from jax.experimental import pallas as pl
from jax.experimental.pallas import tpu as pltpu

F32 = jnp.float32
BF16 = jnp.bfloat16

D_MODEL = 2048
ATTN_WIDTH = D_MODEL // 2
REC_WIDTH = D_MODEL // 2
N_ATTN_HEADS = 8
HEAD_DIM = ATTN_WIDTH // N_ATTN_HEADS
QK_DIM = HEAD_DIM // 2
ROT_HALF = QK_DIM // 2
N_REC_BLOCKS = 8
REC_BLOCK = REC_WIDTH // N_REC_BLOCKS
CONV_WIDTH = 4
LRU_C = 8.0
ROPE_THETA = 10000.0
NORM_EPS = 1e-6
IN_WIDTH = 4 * ATTN_WIDTH + 2 * REC_WIDTH
LAM_INIT = 0.8 - 0.6 * math.exp(-0.3 * 0)
QK_SCALE = QK_DIM ** -0.5
LOG2E = math.log2(math.e)

SUBLANES = 8
LANES = 128

TM_IN = 512
TN_IN = 1024
TQ = 256
SOFTMAX_ROWS = 64
TS = 256
TM_OUT = 512
VMEM_LIMIT = 48 * 1024 * 1024


def _rope_partner(blk, lane):
    fwd = pltpu.roll(blk, LANES - ROT_HALF, axis=1)
    bwd = pltpu.roll(blk, ROT_HALF, axis=1)
    return jnp.where((lane % QK_DIM) < ROT_HALF, fwd, bwd)


def _inproj_kernel(x_ref, pos_ref, invf_ref, gain_ref, w_ref, o_ref,
                   h_scr, cos_scr, sin_scr):
    j = pl.program_id(1)
    tm = x_ref.shape[0]

    @pl.when(j == 0)
    def _():
        x = x_ref[...]
        ms = jnp.mean(x * x, axis=-1, keepdims=True)
        h = x * lax.rsqrt(ms + NORM_EPS) * gain_ref[...]
        h_scr[...] = h.astype(BF16)
        ang = pos_ref[...].astype(F32) * invf_ref[...]
        lane = lax.broadcasted_iota(jnp.int32, ang.shape, 1)
        sign = jnp.where((lane % QK_DIM) < ROT_HALF, -1.0, 1.0).astype(F32)
        cos_scr[...] = jnp.cos(ang)
        sin_scr[...] = jnp.sin(ang) * sign

    @pl.when(j >= 2)
    def _():
        o_ref[...] = jnp.dot(h_scr[...], w_ref[...],
                             preferred_element_type=F32).astype(BF16)

    @pl.when(j < 2)
    def _():
        scale = jnp.where(j == 0, QK_SCALE * LOG2E, 1.0).astype(F32)
        cosf = cos_scr[...] * scale
        sinf = sin_scr[...] * scale
        lane = lax.broadcasted_iota(jnp.int32, (tm, LANES), 1)
        for pair in range(TN_IN // (2 * LANES)):
            c0 = pair * 2 * LANES
            acc = jnp.dot(h_scr[...], w_ref[:, c0:c0 + 2 * LANES],
                          preferred_element_type=F32)
            for half in range(2):
                blk = acc[:, half * LANES:(half + 1) * LANES]
                rot = blk * cosf + _rope_partner(blk, lane) * sinf
                o_ref[:, c0 + half * LANES:c0 + (half + 1) * LANES] = rot.astype(BF16)


def _inproj(x2, pos2, invf, gain, w_in_bf16):
    m = x2.shape[0]
    grid = (m // TM_IN, IN_WIDTH // TN_IN)
    return pl.pallas_call(
        _inproj_kernel,
        grid=grid,
        in_specs=[
            pl.BlockSpec((TM_IN, D_MODEL), lambda i, j: (i, 0)),
            pl.BlockSpec((TM_IN, 1), lambda i, j: (i, 0)),
            pl.BlockSpec((1, LANES), lambda i, j: (0, 0)),
            pl.BlockSpec((1, D_MODEL), lambda i, j: (0, 0)),
            pl.BlockSpec((D_MODEL, TN_IN), lambda i, j: (0, j)),
        ],
        out_specs=pl.BlockSpec((TM_IN, TN_IN), lambda i, j: (i, j)),
        out_shape=jax.ShapeDtypeStruct((m, IN_WIDTH), BF16),
        scratch_shapes=[
            pltpu.VMEM((TM_IN, D_MODEL), BF16),
            pltpu.VMEM((TM_IN, LANES), F32),
            pltpu.VMEM((TM_IN, LANES), F32),
        ],
        compiler_params=pltpu.CompilerParams(
            dimension_semantics=("arbitrary", "arbitrary"),
            vmem_limit_bytes=VMEM_LIMIT),
        name="inproj_rope",
    )(x2, pos2, invf, gain, w_in_bf16)


def _dot_nt(a, b):
    return lax.dot_general(a, b, (((1,), (1,)), ((), ())), preferred_element_type=F32)


def _attn_kernel(q_ref, k_ref, v_ref, g_ref, lq1_ref, lk1_ref, lq2_ref, lk2_ref,
                 sg_ref, o_ref, s_scr, p_scr):
    seq = q_ref.shape[0]
    lam = (jnp.exp(jnp.sum(lq1_ref[...] * lk1_ref[...], axis=-1, keepdims=True))
           - jnp.exp(jnp.sum(lq2_ref[...] * lk2_ref[...], axis=-1, keepdims=True))
           + LAM_INIT)
    lane = lax.broadcasted_iota(jnp.int32, (TQ, LANES), 1)
    row = lax.broadcasted_iota(jnp.int32, (2 * TQ, TQ), 0)
    col = lax.broadcasted_iota(jnp.int32, (2 * TQ, TQ), 1)
    causal = col <= jnp.where(row >= TQ, row - TQ, row)

    for i in range(seq // TQ):
        buf = i % 2
        w0, w = i * TQ, (i + 1) * TQ
        q = q_ref[w0:w, :]
        zero = jnp.zeros_like(q)
        qcat = jnp.concatenate([jnp.where(lane < QK_DIM, q, zero),
                                jnp.where(lane >= QK_DIM, q, zero)], axis=0)
        if i > 0:
            s_scr[buf, :, 0:w0] = _dot_nt(qcat, k_ref[0:w0, :])
        s_scr[buf, :, w0:w] = jnp.where(causal, _dot_nt(qcat, k_ref[w0:w, :]), -jnp.inf)

        l_parts = []
        for rb in range(2 * TQ // SOFTMAX_ROWS):
            rows = slice(rb * SOFTMAX_ROWS, (rb + 1) * SOFTMAX_ROWS)
            m = jnp.max(s_scr[buf, rows, 0:w], axis=1, keepdims=True)
            p = jnp.exp2(s_scr[buf, rows, 0:w] - m)
            l_parts.append(jnp.sum(p, axis=1, keepdims=True))
            p_scr[buf, rows, 0:w] = p.astype(BF16)
        inv = 1.0 / jnp.concatenate(l_parts, axis=0)

        acc = jnp.dot(p_scr[buf, :, 0:w], v_ref[0:w, :], preferred_element_type=F32)
        o = acc[:TQ] * inv[:TQ] - acc[TQ:] * (inv[TQ:] * lam)
        ms = jnp.mean(o * o, axis=-1, keepdims=True)
        o = o * lax.rsqrt(ms + NORM_EPS) * sg_ref[...] * (1.0 - LAM_INIT)
        g = g_ref[w0:w, :].astype(F32)
        o_ref[w0:w, :] = (o * (g * jax.nn.sigmoid(g))).astype(BF16)


def _attention(proj, lq1, lk1, lq2, lk2, subln, bsz, seq):
    head_block = (seq, HEAD_DIM)
    q_group, k_group, v_group, g_group = (g * N_ATTN_HEADS for g in range(4))
    small = lambda w: pl.BlockSpec((1, w), lambda b, h: (0, 0))
    return pl.pallas_call(
        _attn_kernel,
        grid=(bsz, N_ATTN_HEADS),
        in_specs=[
            pl.BlockSpec(head_block, lambda b, h: (b, q_group + h)),
            pl.BlockSpec(head_block, lambda b, h: (b, k_group + h)),
            pl.BlockSpec(head_block, lambda b, h: (b, v_group + h)),
            pl.BlockSpec(head_block, lambda b, h: (b, g_group + h)),
            small(QK_DIM), small(QK_DIM), small(QK_DIM), small(QK_DIM),
            small(HEAD_DIM),
        ],
        out_specs=pl.BlockSpec(head_block, lambda b, h: (b, h)),
        out_shape=jax.ShapeDtypeStruct((bsz * seq, ATTN_WIDTH), BF16),
        scratch_shapes=[pltpu.VMEM((2, 2 * TQ, seq), F32),
                        pltpu.VMEM((2, 2 * TQ, seq), BF16)],
        compiler_params=pltpu.CompilerParams(
            dimension_semantics=("arbitrary", "arbitrary"),
            vmem_limit_bytes=VMEM_LIMIT),
        name="diff_attention",
    )(proj, proj, proj, proj, lq1, lk1, lq2, lk2, subln)


def _rglru_kernel(xr_ref, gr_ref, cw_ref, cb_ref, wg_ref, ba_ref, bx_ref, lam_ref,
                  o_ref, xe_scr, a_scr, u_scr, h_scr, hc_scr):
    t = pl.program_id(1)
    halo = SUBLANES

    @pl.when(t == 0)
    def _():
        xe_scr[0:halo, :] = jnp.zeros((halo, REC_WIDTH), F32)
        hc_scr[...] = jnp.zeros_like(hc_scr)

    @pl.when(t > 0)
    def _():
        xe_scr[0:halo, :] = xe_scr[TS:TS + halo, :]

    xe_scr[halo:halo + TS, :] = xr_ref[...].astype(F32)

    groups = TS // SUBLANES
    sub = lax.broadcasted_iota(jnp.int32, (groups, SUBLANES, REC_BLOCK), 1)
    for n in range(N_REC_BLOCKS):
        cs = slice(n * REC_BLOCK, (n + 1) * REC_BLOCK)
        y = cb_ref[:, cs]
        for tap in range(CONV_WIDTH):
            shift = CONV_WIDTH - 1 - tap
            y = y + cw_ref[tap:tap + 1, cs] * xe_scr[halo - shift:halo - shift + TS, cs]
        gates = jnp.dot(y.astype(BF16), wg_ref[n], preferred_element_type=F32)
        r = jax.nn.sigmoid(gates[:, :REC_BLOCK] + ba_ref[:, cs])
        ig = jax.nn.sigmoid(gates[:, REC_BLOCK:] + bx_ref[:, cs])
        log_a = r * (-LRU_C * jax.nn.softplus(-lam_ref[:, cs]))
        a = jnp.exp(log_a)
        mult = jnp.sqrt(-jnp.tanh(log_a) * (a * a + 1.0))
        u = mult * (ig * y)
        a3 = a.reshape(groups, SUBLANES, REC_BLOCK)
        u3 = u.reshape(groups, SUBLANES, REC_BLOCK)
        d = 1
        while d < SUBLANES:
            keep = sub >= d
            a_prev = pltpu.roll(a3, d, axis=1)
            u_prev = pltpu.roll(u3, d, axis=1)
            u3 = jnp.where(keep, a3 * u_prev, 0.0) + u3
            a3 = jnp.where(keep, a3 * a_prev, a3)
            d *= 2
        a_scr[:, cs] = a3.reshape(TS, REC_BLOCK)
        u_scr[:, cs] = u3.reshape(TS, REC_BLOCK)

    h_prev = hc_scr[...]
    for g in range(groups):
        rows = slice(g * SUBLANES, (g + 1) * SUBLANES)
        hg = u_scr[rows, :] + a_scr[rows, :] * h_prev
        h_scr[rows, :] = hg
        h_prev = jnp.broadcast_to(hg[SUBLANES - 1:SUBLANES, :], hg.shape)
    hc_scr[...] = h_prev

    gr = gr_ref[...].astype(F32)
    o_ref[...] = (h_scr[...] * (gr * jax.nn.sigmoid(gr))).astype(BF16)


def _rglru(proj, conv_w, conv_b, w_gate, b_a, b_x, lru_lambda, bsz, seq):
    n_t = seq // TS
    xr_group, gr_group = 4, 5
    full = lambda shape: pl.BlockSpec(shape, lambda b, t: (0,) * len(shape))
    return pl.pallas_call(
        _rglru_kernel,
        grid=(bsz, n_t),
        in_specs=[
            pl.BlockSpec((TS, REC_WIDTH), lambda b, t: (b * n_t + t, xr_group)),
            pl.BlockSpec((TS, REC_WIDTH), lambda b, t: (b * n_t + t, gr_group)),
            full((CONV_WIDTH, REC_WIDTH)),
            full((1, REC_WIDTH)),
            full((N_REC_BLOCKS, REC_BLOCK, 2 * REC_BLOCK)),
            full((1, REC_WIDTH)), full((1, REC_WIDTH)), full((1, REC_WIDTH)),
        ],
        out_specs=pl.BlockSpec((TS, REC_WIDTH), lambda b, t: (b * n_t + t, 0)),
        out_shape=jax.ShapeDtypeStruct((bsz * seq, REC_WIDTH), BF16),
        scratch_shapes=[
            pltpu.VMEM((TS + SUBLANES, REC_WIDTH), F32),
            pltpu.VMEM((TS, REC_WIDTH), F32),
            pltpu.VMEM((TS, REC_WIDTH), F32),
            pltpu.VMEM((TS, REC_WIDTH), F32),
            pltpu.VMEM((SUBLANES, REC_WIDTH), F32),
        ],
        compiler_params=pltpu.CompilerParams(
            dimension_semantics=("arbitrary", "arbitrary"),
            vmem_limit_bytes=VMEM_LIMIT),
        name="conv_rglru",
    )(proj, proj, conv_w, conv_b, w_gate, b_a, b_x, lru_lambda)


def _outproj_kernel(x_ref, o_ref, r_ref, w_ref, gain_ref, y_ref):
    y = x_ref[...]
    y = y + jnp.dot(o_ref[...], w_ref[0:ATTN_WIDTH, :], preferred_element_type=F32)
    y = y + jnp.dot(r_ref[...], w_ref[ATTN_WIDTH:, :], preferred_element_type=F32)
    ms = jnp.mean(y * y, axis=-1, keepdims=True)
    y_ref[...] = y * lax.rsqrt(ms + NORM_EPS) * gain_ref[...]


def _outproj(x2, o, r, w_out_bf16, final_gain):
    m = x2.shape[0]
    return pl.pallas_call(
        _outproj_kernel,
        grid=(m // TM_OUT,),
        in_specs=[
            pl.BlockSpec((TM_OUT, D_MODEL), lambda i: (i, 0)),
            pl.BlockSpec((TM_OUT, ATTN_WIDTH), lambda i: (i, 0)),
            pl.BlockSpec((TM_OUT, REC_WIDTH), lambda i: (i, 0)),
            pl.BlockSpec((ATTN_WIDTH + REC_WIDTH, D_MODEL), lambda i: (0, 0)),
            pl.BlockSpec((1, D_MODEL), lambda i: (0, 0)),
        ],
        out_specs=pl.BlockSpec((TM_OUT, D_MODEL), lambda i: (i, 0)),
        out_shape=jax.ShapeDtypeStruct((m, D_MODEL), F32),
        compiler_params=pltpu.CompilerParams(
            dimension_semantics=("arbitrary",),
            vmem_limit_bytes=VMEM_LIMIT),
        name="outproj_norm",
    )(x2, o, r, w_out_bf16, final_gain)


def kernel(x, positions, norm_gain, w_in, lambda_q1, lambda_k1, lambda_q2, lambda_k2,
           subln_gain, conv_w, conv_b, w_a, b_a, w_x, b_x, lru_lambda, w_out, final_gain):
    bsz, seq, _ = x.shape
    assert x.shape[2] == D_MODEL and norm_gain.shape[0] == 1
    assert seq % max(TQ, TS) == 0 and (bsz * seq) % max(TM_IN, TM_OUT) == 0
    x2 = x.reshape(bsz * seq, D_MODEL)
    pos2 = positions.reshape(bsz * seq, 1)
    inv_freq = ROPE_THETA ** (-jnp.arange(0, QK_DIM, 2, dtype=F32) / QK_DIM)
    invf = jnp.tile(inv_freq, LANES // ROT_HALF).reshape(1, LANES)

    proj = _inproj(x2, pos2, invf, norm_gain[0].reshape(1, D_MODEL),
                   w_in[0].astype(BF16))
    o = _attention(proj, lambda_q1[0].reshape(1, QK_DIM), lambda_k1[0].reshape(1, QK_DIM),
                   lambda_q2[0].reshape(1, QK_DIM), lambda_k2[0].reshape(1, QK_DIM),
                   subln_gain[0].reshape(1, HEAD_DIM), bsz, seq)
    w_gate = jnp.concatenate([w_a[0], w_x[0]], axis=-1).astype(BF16)
    r = _rglru(proj, conv_w[0], conv_b[0].reshape(1, REC_WIDTH), w_gate,
               b_a[0].reshape(1, REC_WIDTH), b_x[0].reshape(1, REC_WIDTH),
               lru_lambda[0].reshape(1, REC_WIDTH), bsz, seq)
    y = _outproj(x2, o, r, w_out[0].astype(BF16), final_gain.reshape(1, D_MODEL))
    return y.reshape(bsz, seq, D_MODEL)
```

```python
import math

import jax
import jax.numpy as jnp
from jax import lax
from jax.experimental import pallas as pl
from jax.experimental.pallas import tpu as pltpu

F32 = jnp.float32
BF16 = jnp.bfloat16

D_MODEL = 2048
ATTN_WIDTH = D_MODEL // 2
REC_WIDTH = D_MODEL // 2
N_ATTN_HEADS = 8
HEAD_DIM = ATTN_WIDTH // N_ATTN_HEADS
QK_DIM = HEAD_DIM // 2
ROT_HALF = QK_DIM // 2
N_REC_BLOCKS = 8
REC_BLOCK = REC_WIDTH // N_REC_BLOCKS
CONV_WIDTH = 4
LRU_C = 8.0
ROPE_THETA = 10000.0
NORM_EPS = 1e-6
IN_WIDTH = 4 * ATTN_WIDTH + 2 * REC_WIDTH
LAM_INIT = 0.8 - 0.6 * math.exp(-0.3 * 0)
QK_SCALE = QK_DIM ** -0.5
LOG2E = math.log2(math.e)

SUBLANES = 8
LANES = 128

TM_IN = 512
TN_IN = 1024
TQ = 256
SOFTMAX_ROWS = 64
TS = 256
TM_OUT = 512
VMEM_LIMIT = 48 * 1024 * 1024


N_COL_TILES = IN_WIDTH // TN_IN
N_ROPE_TILES = 2 * ATTN_WIDTH // TN_IN
N_PLAIN_TILES = N_COL_TILES - N_ROPE_TILES


def _col_tile(j):
    return (j + N_ROPE_TILES) % N_COL_TILES


def _inproj_kernel(x_ref, pos_ref, invf_ref, gain_ref, w_ref, o_ref,
                   h_scr, rstd_scr, cos_scr, sin_scr):
    j = pl.program_id(1)
    tm = x_ref.shape[0]

    def plain_store(h, rstd):
        rstd2 = jnp.concatenate([rstd, rstd], axis=1)
        for pair in range(TN_IN // (2 * LANES)):
            cs = slice(pair * 2 * LANES, (pair + 1) * 2 * LANES)
            acc = jnp.dot(h, w_ref[:, cs], preferred_element_type=F32)
            o_ref[:, cs] = (acc * rstd2).astype(BF16)

    @pl.when(j == 0)
    def _():
        x = x_ref[...]
        h = (x * gain_ref[...]).astype(BF16)
        h_scr[...] = h
        rstd = lax.rsqrt(jnp.mean(x * x, axis=-1, keepdims=True) + NORM_EPS)
        rstd = jnp.broadcast_to(rstd, (tm, LANES))
        rstd_scr[...] = rstd
        plain_store(h, rstd)
        ang = pos_ref[...].astype(F32) * invf_ref[...]
        lane = lax.broadcasted_iota(jnp.int32, (tm, LANES), 1)
        cos_scr[...] = jnp.cos(ang)
        sin_scr[...] = jnp.sin(ang) * jnp.where(lane < QK_DIM, -1.0, 1.0).astype(F32)

    @pl.when(jnp.logical_and(j > 0, j < N_PLAIN_TILES))
    def _():
        plain_store(h_scr[...], rstd_scr[...])

    @pl.when(j >= N_PLAIN_TILES)
    def _():
        scale = jnp.where(j == N_PLAIN_TILES, QK_SCALE * LOG2E, 1.0).astype(F32)
        rstd = rstd_scr[...]
        cosf = cos_scr[...] * scale
        sinf = sin_scr[...] * scale
        for pair in range(TN_IN // (2 * LANES)):
            c0 = pair * 2 * LANES
            acc = jnp.dot(h_scr[...], w_ref[:, c0:c0 + 2 * LANES],
                          preferred_element_type=F32)
            for half in range(2):
                blk = acc[:, half * LANES:(half + 1) * LANES] * rstd
                rot = blk * cosf + pltpu.roll(blk, QK_DIM, axis=1) * sinf
                o_ref[:, c0 + half * LANES:c0 + (half + 1) * LANES] = rot.astype(BF16)


def _inproj(x2, pos2, invf, gain, w_in_bf16):
    m = x2.shape[0]
    grid = (m // TM_IN, N_COL_TILES)
    return pl.pallas_call(
        _inproj_kernel,
        grid=grid,
        in_specs=[
            pl.BlockSpec((TM_IN, D_MODEL), lambda i, j: (i, 0)),
            pl.BlockSpec((TM_IN, 1), lambda i, j: (i, 0)),
            pl.BlockSpec((1, LANES), lambda i, j: (0, 0)),
            pl.BlockSpec((1, D_MODEL), lambda i, j: (0, 0)),
            pl.BlockSpec((D_MODEL, TN_IN), lambda i, j: (0, _col_tile(j))),
        ],
        out_specs=pl.BlockSpec((TM_IN, TN_IN), lambda i, j: (i, _col_tile(j))),
        out_shape=jax.ShapeDtypeStruct((m, IN_WIDTH), BF16),
        scratch_shapes=[
            pltpu.VMEM((TM_IN, D_MODEL), BF16),
            pltpu.VMEM((TM_IN, LANES), F32),
            pltpu.VMEM((TM_IN, LANES), F32),
            pltpu.VMEM((TM_IN, LANES), F32),
        ],
        compiler_params=pltpu.CompilerParams(
            dimension_semantics=("arbitrary", "arbitrary"),
            vmem_limit_bytes=VMEM_LIMIT),
        name="inproj_rope",
    )(x2, pos2, invf, gain, w_in_bf16)


def _dot_nt(a, b):
    return lax.dot_general(a, b, (((1,), (1,)), ((), ())), preferred_element_type=F32)


def _attn_kernel(q_ref, k_ref, v_ref, g_ref, lq1_ref, lk1_ref, lq2_ref, lk2_ref,
                 sg_ref, o_ref, s_scr, p_scr):
    seq = q_ref.shape[0]
    lam = (jnp.exp(jnp.sum(lq1_ref[...] * lk1_ref[...], axis=-1, keepdims=True))
           - jnp.exp(jnp.sum(lq2_ref[...] * lk2_ref[...], axis=-1, keepdims=True))
           + LAM_INIT)
    lane = lax.broadcasted_iota(jnp.int32, (TQ, LANES), 1)
    first_map = (lane % QK_DIM) < ROT_HALF
    row = lax.broadcasted_iota(jnp.int32, (2 * TQ, TQ), 0)
    col = lax.broadcasted_iota(jnp.int32, (2 * TQ, TQ), 1)
    causal = col <= jnp.where(row >= TQ, row - TQ, row)

    for i in range(seq // TQ):
        buf = i % 2
        w0, w = i * TQ, (i + 1) * TQ
        q = q_ref[w0:w, :]
        zero = jnp.zeros_like(q)
        qcat = jnp.concatenate([jnp.where(first_map, q, zero),
                                jnp.where(first_map, zero, q)], axis=0)
        if i > 0:
            s_scr[buf, :, 0:w0] = _dot_nt(qcat, k_ref[0:w0, :])
        s_scr[buf, :, w0:w] = jnp.where(causal, _dot_nt(qcat, k_ref[w0:w, :]), -jnp.inf)

        l_parts = []
        for rb in range(2 * TQ // SOFTMAX_ROWS):
            rows = slice(rb * SOFTMAX_ROWS, (rb + 1) * SOFTMAX_ROWS)
            m = jnp.max(s_scr[buf, rows, 0:w], axis=1, keepdims=True)
            p = jnp.exp2(s_scr[buf, rows, 0:w] - m)
            l_parts.append(jnp.sum(p, axis=1, keepdims=True))
            p_scr[buf, rows, 0:w] = p.astype(BF16)
        inv = 1.0 / jnp.concatenate(l_parts, axis=0)

        acc = jnp.dot(p_scr[buf, :, 0:w], v_ref[0:w, :], preferred_element_type=F32)
        o = acc[:TQ] * inv[:TQ] - acc[TQ:] * (inv[TQ:] * lam)
        ms = jnp.mean(o * o, axis=-1, keepdims=True)
        o = o * lax.rsqrt(ms + NORM_EPS) * sg_ref[...] * (1.0 - LAM_INIT)
        g = g_ref[w0:w, :].astype(F32)
        o_ref[w0:w, :] = (o * (g * jax.nn.sigmoid(g))).astype(BF16)


def _attention(proj, lq1, lk1, lq2, lk2, subln, bsz, seq):
    head_block = (seq, HEAD_DIM)
    q_group, k_group, v_group, g_group = (g * N_ATTN_HEADS for g in range(4))
    small = lambda w: pl.BlockSpec((1, w), lambda b, h: (0, 0))
    return pl.pallas_call(
        _attn_kernel,
        grid=(bsz, N_ATTN_HEADS),
        in_specs=[
            pl.BlockSpec(head_block, lambda b, h: (b, q_group + h)),
            pl.BlockSpec(head_block, lambda b, h: (b, k_group + h)),
            pl.BlockSpec(head_block, lambda b, h: (b, v_group + h)),
            pl.BlockSpec(head_block, lambda b, h: (b, g_group + h)),
            small(QK_DIM), small(QK_DIM), small(QK_DIM), small(QK_DIM),
            small(HEAD_DIM),
        ],
        out_specs=pl.BlockSpec(head_block, lambda b, h: (b, h)),
        out_shape=jax.ShapeDtypeStruct((bsz * seq, ATTN_WIDTH), BF16),
        scratch_shapes=[pltpu.VMEM((2, 2 * TQ, seq), F32),
                        pltpu.VMEM((2, 2 * TQ, seq), BF16)],
        compiler_params=pltpu.CompilerParams(
            dimension_semantics=("arbitrary", "arbitrary"),
            vmem_limit_bytes=VMEM_LIMIT),
        name="diff_attention",
    )(proj, proj, proj, proj, lq1, lk1, lq2, lk2, subln)


def _rglru_kernel(xr_ref, gr_ref, cw_ref, cb_ref, wg_ref, ba_ref, bx_ref, lam_ref,
                  o_ref, xe_scr, a_scr, u_scr, h_scr, hc_scr):
    t = pl.program_id(1)
    halo = SUBLANES

    @pl.when(t == 0)
    def _():
        xe_scr[0:halo, :] = jnp.zeros((halo, REC_WIDTH), F32)
        hc_scr[...] = jnp.zeros_like(hc_scr)

    @pl.when(t > 0)
    def _():
        xe_scr[0:halo, :] = xe_scr[TS:TS + halo, :]

    xe_scr[halo:halo + TS, :] = xr_ref[...].astype(F32)

    groups = TS // SUBLANES
    sub = lax.broadcasted_iota(jnp.int32, (groups, SUBLANES, REC_BLOCK), 1)
    for n in range(N_REC_BLOCKS):
        cs = slice(n * REC_BLOCK, (n + 1) * REC_BLOCK)
        y = cb_ref[:, cs]
        for tap in range(CONV_WIDTH):
            shift = CONV_WIDTH - 1 - tap
            y = y + cw_ref[tap:tap + 1, cs] * xe_scr[halo - shift:halo - shift + TS, cs]
        gates = jnp.dot(y.astype(BF16), wg_ref[n], preferred_element_type=F32)
        r = jax.nn.sigmoid(gates[:, :REC_BLOCK] + ba_ref[:, cs])
        ig = jax.nn.sigmoid(gates[:, REC_BLOCK:] + bx_ref[:, cs])
        log_a = r * (-LRU_C * jax.nn.softplus(-lam_ref[:, cs]))
        a = jnp.exp(log_a)
        mult = jnp.sqrt(-jnp.tanh(log_a) * (a * a + 1.0))
        u = mult * (ig * y)
        a3 = a.reshape(groups, SUBLANES, REC_BLOCK)
        u3 = u.reshape(groups, SUBLANES, REC_BLOCK)
        d = 1
        while d < SUBLANES:
            keep = sub >= d
            a_prev = pltpu.roll(a3, d, axis=1)
            u_prev = pltpu.roll(u3, d, axis=1)
            u3 = jnp.where(keep, a3 * u_prev, 0.0) + u3
            a3 = jnp.where(keep, a3 * a_prev, a3)
            d *= 2
        a_scr[:, cs] = a3.reshape(TS, REC_BLOCK)
        u_scr[:, cs] = u3.reshape(TS, REC_BLOCK)

    h_prev = hc_scr[...]
    for g in range(groups):
        rows = slice(g * SUBLANES, (g + 1) * SUBLANES)
        hg = u_scr[rows, :] + a_scr[rows, :] * h_prev
        h_scr[rows, :] = hg
        h_prev = jnp.broadcast_to(hg[SUBLANES - 1:SUBLANES, :], hg.shape)
    hc_scr[...] = h_prev

    gr = gr_ref[...].astype(F32)
    o_ref[...] = (h_scr[...] * (gr * jax.nn.sigmoid(gr))).astype(BF16)


def _rglru(proj, conv_w, conv_b, w_gate, b_a, b_x, lru_lambda, bsz, seq):
    n_t = seq // TS
    xr_group, gr_group = 4, 5
    full = lambda shape: pl.BlockSpec(shape, lambda b, t: (0,) * len(shape))
    return pl.pallas_call(
        _rglru_kernel,
        grid=(bsz, n_t),
        in_specs=[
            pl.BlockSpec((TS, REC_WIDTH), lambda b, t: (b * n_t + t, xr_group)),
            pl.BlockSpec((TS, REC_WIDTH), lambda b, t: (b * n_t + t, gr_group)),
            full((CONV_WIDTH, REC_WIDTH)),
            full((1, REC_WIDTH)),
            full((N_REC_BLOCKS, REC_BLOCK, 2 * REC_BLOCK)),
            full((1, REC_WIDTH)), full((1, REC_WIDTH)), full((1, REC_WIDTH)),
        ],
        out_specs=pl.BlockSpec((TS, REC_WIDTH), lambda b, t: (b * n_t + t, 0)),
        out_shape=jax.ShapeDtypeStruct((bsz * seq, REC_WIDTH), BF16),
        scratch_shapes=[
            pltpu.VMEM((TS + SUBLANES, REC_WIDTH), F32),
            pltpu.VMEM((TS, REC_WIDTH), F32),
            pltpu.VMEM((TS, REC_WIDTH), F32),
            pltpu.VMEM((TS, REC_WIDTH), F32),
            pltpu.VMEM((SUBLANES, REC_WIDTH), F32),
        ],
        compiler_params=pltpu.CompilerParams(
            dimension_semantics=("arbitrary", "arbitrary"),
            vmem_limit_bytes=VMEM_LIMIT),
        name="conv_rglru",
    )(proj, proj, conv_w, conv_b, w_gate, b_a, b_x, lru_lambda)


def _outproj_kernel(x_ref, o_ref, r_ref, w_ref, gain_ref, y_ref):
    y = x_ref[...]
    y = y + jnp.dot(o_ref[...], w_ref[0:ATTN_WIDTH, :], preferred_element_type=F32)
    y = y + jnp.dot(r_ref[...], w_ref[ATTN_WIDTH:, :], preferred_element_type=F32)
    ms = jnp.mean(y * y, axis=-1, keepdims=True)
    y_ref[...] = y * lax.rsqrt(ms + NORM_EPS) * gain_ref[...]


def _outproj(x2, o, r, w_out_bf16, final_gain):
    m = x2.shape[0]
    return pl.pallas_call(
        _outproj_kernel,
        grid=(m // TM_OUT,),
        in_specs=[
            pl.BlockSpec((TM_OUT, D_MODEL), lambda i: (i, 0)),
            pl.BlockSpec((TM_OUT, ATTN_WIDTH), lambda i: (i, 0)),
            pl.BlockSpec((TM_OUT, REC_WIDTH), lambda i: (i, 0)),
            pl.BlockSpec((ATTN_WIDTH + REC_WIDTH, D_MODEL), lambda i: (0, 0)),
            pl.BlockSpec((1, D_MODEL), lambda i: (0, 0)),
        ],
        out_specs=pl.BlockSpec((TM_OUT, D_MODEL), lambda i: (i, 0)),
        out_shape=jax.ShapeDtypeStruct((m, D_MODEL), F32),
        compiler_params=pltpu.CompilerParams(
            dimension_semantics=("arbitrary",),
            vmem_limit_bytes=VMEM_LIMIT),
        name="outproj_norm",
    )(x2, o, r, w_out_bf16, final_gain)


def _rotary_column_order(w):
    d = w.shape[0]
    qk = w[:, :2 * ATTN_WIDTH].reshape(d, 2 * N_ATTN_HEADS, 2, 2, ROT_HALF)
    qk = qk.transpose(0, 1, 3, 2, 4).reshape(d, 2 * ATTN_WIDTH)
    return jnp.concatenate([qk, w[:, 2 * ATTN_WIDTH:]], axis=1)


def kernel(x, positions, norm_gain, w_in, lambda_q1, lambda_k1, lambda_q2, lambda_k2,
           subln_gain, conv_w, conv_b, w_a, b_a, w_x, b_x, lru_lambda, w_out, final_gain):
    bsz, seq, _ = x.shape
    assert x.shape[2] == D_MODEL and norm_gain.shape[0] == 1
    assert seq % max(TQ, TS) == 0 and (bsz * seq) % max(TM_IN, TM_OUT) == 0
    x2 = x.reshape(bsz * seq, D_MODEL)
    pos2 = positions.reshape(bsz * seq, 1)
    inv_freq = ROPE_THETA ** (-jnp.arange(0, QK_DIM, 2, dtype=F32) / QK_DIM)
    invf = jnp.tile(inv_freq, LANES // ROT_HALF).reshape(1, LANES)

    proj = _inproj(x2, pos2, invf, norm_gain[0].reshape(1, D_MODEL),
                   _rotary_column_order(w_in[0]).astype(BF16))
    o = _attention(proj, lambda_q1[0].reshape(1, QK_DIM), lambda_k1[0].reshape(1, QK_DIM),
                   lambda_q2[0].reshape(1, QK_DIM), lambda_k2[0].reshape(1, QK_DIM),
                   subln_gain[0].reshape(1, HEAD_DIM), bsz, seq)
    w_gate = jnp.concatenate([w_a[0], w_x[0]], axis=-1).astype(BF16)
    r = _rglru(proj, conv_w[0], conv_b[0].reshape(1, REC_WIDTH), w_gate,
               b_a[0].reshape(1, REC_WIDTH), b_x[0].reshape(1, REC_WIDTH),
               lru_lambda[0].reshape(1, REC_WIDTH), bsz, seq)
    y = _outproj(x2, o, r, w_out[0].astype(BF16), final_gain.reshape(1, D_MODEL))
    return y.reshape(bsz, seq, D_MODEL)
```

```python
import math

import jax
import jax.numpy as jnp
from jax import lax
from jax.experimental import pallas as pl
from jax.experimental.pallas import tpu as pltpu

F32 = jnp.float32
BF16 = jnp.bfloat16

D_MODEL = 2048
ATTN_WIDTH = D_MODEL // 2
REC_WIDTH = D_MODEL // 2
N_ATTN_HEADS = 8
HEAD_DIM = ATTN_WIDTH // N_ATTN_HEADS
QK_DIM = HEAD_DIM // 2
ROT_HALF = QK_DIM // 2
N_REC_BLOCKS = 8
REC_BLOCK = REC_WIDTH // N_REC_BLOCKS
CONV_WIDTH = 4
LRU_C = 8.0
ROPE_THETA = 10000.0
NORM_EPS = 1e-6
IN_WIDTH = 4 * ATTN_WIDTH + 2 * REC_WIDTH
LAM_INIT = 0.8 - 0.6 * math.exp(-0.3 * 0)
QK_SCALE = QK_DIM ** -0.5
LOG2E = math.log2(math.e)

SUBLANES = 8
LANES = 128

TM_IN = 512
TN_IN = 1024
TQ = 256
SOFTMAX_ROWS = 64
TS = 256
TM_OUT = 512
VMEM_LIMIT = 48 * 1024 * 1024
VMEM_LIMIT_INPROJ = 58 * 1024 * 1024


N_COL_TILES = IN_WIDTH // TN_IN
N_ROPE_TILES = 2 * ATTN_WIDTH // TN_IN
N_PLAIN_TILES = N_COL_TILES - N_ROPE_TILES


def _col_tile(j):
    return (j + N_ROPE_TILES) % N_COL_TILES


def _rope_partner(blk, lane):
    fwd = pltpu.roll(blk, LANES - ROT_HALF, axis=1)
    bwd = pltpu.roll(blk, ROT_HALF, axis=1)
    return jnp.where((lane % QK_DIM) < ROT_HALF, fwd, bwd)


def _inproj_kernel(x_ref, pos_ref, invf_ref, gain_ref, w_ref, o_ref,
                   w_scr, h_scr, rstd_scr, cos_scr, sin_scr):
    i = pl.program_id(0)
    j = pl.program_id(1)
    tm = x_ref.shape[0]

    @pl.when(i == 0)
    def _():
        w_scr[j] = w_ref[...].astype(BF16)

    def plain_store(h, rstd):
        rstd2 = jnp.concatenate([rstd, rstd], axis=1)
        for pair in range(TN_IN // (2 * LANES)):
            cs = slice(pair * 2 * LANES, (pair + 1) * 2 * LANES)
            acc = jnp.dot(h, w_scr[j, :, cs], preferred_element_type=F32)
            o_ref[:, cs] = (acc * rstd2).astype(BF16)

    @pl.when(j == 0)
    def _():
        x = x_ref[...]
        h = (x * gain_ref[...]).astype(BF16)
        h_scr[...] = h
        rstd = lax.rsqrt(jnp.mean(x * x, axis=-1, keepdims=True) + NORM_EPS)
        rstd = jnp.broadcast_to(rstd, (tm, LANES))
        rstd_scr[...] = rstd
        plain_store(h, rstd)
        ang = pos_ref[...].astype(F32) * invf_ref[...]
        lane = lax.broadcasted_iota(jnp.int32, (tm, LANES), 1)
        sign = jnp.where((lane % QK_DIM) < ROT_HALF, -1.0, 1.0).astype(F32)
        cos_scr[...] = jnp.cos(ang)
        sin_scr[...] = jnp.sin(ang) * sign

    @pl.when(jnp.logical_and(j > 0, j < N_PLAIN_TILES))
    def _():
        plain_store(h_scr[...], rstd_scr[...])

    @pl.when(j >= N_PLAIN_TILES)
    def _():
        scale = jnp.where(j == N_PLAIN_TILES, QK_SCALE * LOG2E, 1.0).astype(F32)
        rstd = rstd_scr[...]
        cosf = cos_scr[...] * scale
        sinf = sin_scr[...] * scale
        lane = lax.broadcasted_iota(jnp.int32, (tm, LANES), 1)
        for pair in range(TN_IN // (2 * LANES)):
            c0 = pair * 2 * LANES
            acc = jnp.dot(h_scr[...], w_scr[j, :, c0:c0 + 2 * LANES],
                          preferred_element_type=F32)
            for half in range(2):
                blk = acc[:, half * LANES:(half + 1) * LANES] * rstd
                rot = blk * cosf + _rope_partner(blk, lane) * sinf
                o_ref[:, c0 + half * LANES:c0 + (half + 1) * LANES] = rot.astype(BF16)


def _inproj(x2, pos2, invf, gain, w_in):
    m = x2.shape[0]
    grid = (m // TM_IN, N_COL_TILES)
    last_tile = _col_tile(N_COL_TILES - 1)
    return pl.pallas_call(
        _inproj_kernel,
        grid=grid,
        in_specs=[
            pl.BlockSpec((TM_IN, D_MODEL), lambda i, j: (i, 0)),
            pl.BlockSpec((TM_IN, 1), lambda i, j: (i, 0)),
            pl.BlockSpec((1, LANES), lambda i, j: (0, 0)),
            pl.BlockSpec((1, D_MODEL), lambda i, j: (0, 0)),
            pl.BlockSpec((D_MODEL, TN_IN),
                         lambda i, j: (0, jnp.where(i == 0, _col_tile(j), last_tile))),
        ],
        out_specs=pl.BlockSpec((TM_IN, TN_IN), lambda i, j: (i, _col_tile(j))),
        out_shape=jax.ShapeDtypeStruct((m, IN_WIDTH), BF16),
        scratch_shapes=[
            pltpu.VMEM((N_COL_TILES, D_MODEL, TN_IN), BF16),
            pltpu.VMEM((TM_IN, D_MODEL), BF16),
            pltpu.VMEM((TM_IN, LANES), F32),
            pltpu.VMEM((TM_IN, LANES), F32),
            pltpu.VMEM((TM_IN, LANES), F32),
        ],
        compiler_params=pltpu.CompilerParams(
            dimension_semantics=("arbitrary", "arbitrary"),
            vmem_limit_bytes=VMEM_LIMIT_INPROJ),
        name="inproj_rope",
    )(x2, pos2, invf, gain, w_in)


def _dot_nt(a, b):
    return lax.dot_general(a, b, (((1,), (1,)), ((), ())), preferred_element_type=F32)


def _attn_kernel(q_ref, k_ref, v_ref, g_ref, lq1_ref, lk1_ref, lq2_ref, lk2_ref,
                 sg_ref, o_ref, s_scr, p_scr):
    seq = q_ref.shape[0]
    lam = (jnp.exp(jnp.sum(lq1_ref[...] * lk1_ref[...], axis=-1, keepdims=True))
           - jnp.exp(jnp.sum(lq2_ref[...] * lk2_ref[...], axis=-1, keepdims=True))
           + LAM_INIT)
    lane = lax.broadcasted_iota(jnp.int32, (TQ, LANES), 1)
    row = lax.broadcasted_iota(jnp.int32, (2 * TQ, TQ), 0)
    col = lax.broadcasted_iota(jnp.int32, (2 * TQ, TQ), 1)
    causal = col <= jnp.where(row >= TQ, row - TQ, row)

    for i in range(seq // TQ):
        buf = i % 2
        w0, w = i * TQ, (i + 1) * TQ
        q = q_ref[w0:w, :]
        zero = jnp.zeros_like(q)
        qcat = jnp.concatenate([jnp.where(lane < QK_DIM, q, zero),
                                jnp.where(lane < QK_DIM, zero, q)], axis=0)
        if i > 0:
            s_scr[buf, :, 0:w0] = _dot_nt(qcat, k_ref[0:w0, :])
        s_scr[buf, :, w0:w] = jnp.where(causal, _dot_nt(qcat, k_ref[w0:w, :]), -jnp.inf)

        l_parts = []
        for rb in range(2 * TQ // SOFTMAX_ROWS):
            rows = slice(rb * SOFTMAX_ROWS, (rb + 1) * SOFTMAX_ROWS)
            m = jnp.max(s_scr[buf, rows, 0:w], axis=1, keepdims=True)
            p = jnp.exp2(s_scr[buf, rows, 0:w] - m)
            l_parts.append(jnp.sum(p, axis=1, keepdims=True))
            p_scr[buf, rows, 0:w] = p.astype(BF16)
        inv = 1.0 / jnp.concatenate(l_parts, axis=0)

        acc = jnp.dot(p_scr[buf, :, 0:w], v_ref[0:w, :], preferred_element_type=F32)
        o = acc[:TQ] * inv[:TQ] - acc[TQ:] * (inv[TQ:] * lam)
        ms = jnp.mean(o * o, axis=-1, keepdims=True)
        o = o * lax.rsqrt(ms + NORM_EPS) * sg_ref[...] * (1.0 - LAM_INIT)
        g = g_ref[w0:w, :].astype(F32)
        o_ref[w0:w, :] = (o * (g * jax.nn.sigmoid(g))).astype(BF16)


def _attention(proj, lq1, lk1, lq2, lk2, subln, bsz, seq):
    head_block = (seq, HEAD_DIM)
    q_group, k_group, v_group, g_group = (g * N_ATTN_HEADS for g in range(4))
    small = lambda w: pl.BlockSpec((1, w), lambda b, h: (0, 0))
    return pl.pallas_call(
        _attn_kernel,
        grid=(bsz, N_ATTN_HEADS),
        in_specs=[
            pl.BlockSpec(head_block, lambda b, h: (b, q_group + h)),
            pl.BlockSpec(head_block, lambda b, h: (b, k_group + h)),
            pl.BlockSpec(head_block, lambda b, h: (b, v_group + h)),
            pl.BlockSpec(head_block, lambda b, h: (b, g_group + h)),
            small(QK_DIM), small(QK_DIM), small(QK_DIM), small(QK_DIM),
            small(HEAD_DIM),
        ],
        out_specs=pl.BlockSpec(head_block, lambda b, h: (b, h)),
        out_shape=jax.ShapeDtypeStruct((bsz * seq, ATTN_WIDTH), BF16),
        scratch_shapes=[pltpu.VMEM((2, 2 * TQ, seq), F32),
                        pltpu.VMEM((2, 2 * TQ, seq), BF16)],
        compiler_params=pltpu.CompilerParams(
            dimension_semantics=("arbitrary", "arbitrary"),
            vmem_limit_bytes=VMEM_LIMIT),
        name="diff_attention",
    )(proj, proj, proj, proj, lq1, lk1, lq2, lk2, subln)


def _rglru_kernel(xr_ref, gr_ref, cw_ref, cb_ref, wg_ref, ba_ref, bx_ref, lam_ref,
                  o_ref, xe_scr, a_scr, u_scr, h_scr, hc_scr):
    t = pl.program_id(1)
    halo = SUBLANES

    @pl.when(t == 0)
    def _():
        xe_scr[0:halo, :] = jnp.zeros((halo, REC_WIDTH), F32)
        hc_scr[...] = jnp.zeros_like(hc_scr)

    @pl.when(t > 0)
    def _():
        xe_scr[0:halo, :] = xe_scr[TS:TS + halo, :]

    xe_scr[halo:halo + TS, :] = xr_ref[...].astype(F32)

    groups = TS // SUBLANES
    sub = lax.broadcasted_iota(jnp.int32, (groups, SUBLANES, REC_BLOCK), 1)
    for n in range(N_REC_BLOCKS):
        cs = slice(n * REC_BLOCK, (n + 1) * REC_BLOCK)
        y = cb_ref[:, cs]
        for tap in range(CONV_WIDTH):
            shift = CONV_WIDTH - 1 - tap
            y = y + cw_ref[tap:tap + 1, cs] * xe_scr[halo - shift:halo - shift + TS, cs]
        gates = jnp.dot(y.astype(BF16), wg_ref[n], preferred_element_type=F32)
        r = jax.nn.sigmoid(gates[:, :REC_BLOCK] + ba_ref[:, cs])
        ig = jax.nn.sigmoid(gates[:, REC_BLOCK:] + bx_ref[:, cs])
        log_a = r * (-LRU_C * jax.nn.softplus(-lam_ref[:, cs]))
        a = jnp.exp(log_a)
        mult = jnp.sqrt(-jnp.tanh(log_a) * (a * a + 1.0))
        u = mult * (ig * y)
        a3 = a.reshape(groups, SUBLANES, REC_BLOCK)
        u3 = u.reshape(groups, SUBLANES, REC_BLOCK)
        d = 1
        while d < SUBLANES:
            keep = sub >= d
            a_prev = pltpu.roll(a3, d, axis=1)
            u_prev = pltpu.roll(u3, d, axis=1)
            u3 = jnp.where(keep, a3 * u_prev, 0.0) + u3
            a3 = jnp.where(keep, a3 * a_prev, a3)
            d *= 2
        a_scr[:, cs] = a3.reshape(TS, REC_BLOCK)
        u_scr[:, cs] = u3.reshape(TS, REC_BLOCK)

    h_prev = hc_scr[...]
    for g in range(groups):
        rows = slice(g * SUBLANES, (g + 1) * SUBLANES)
        hg = u_scr[rows, :] + a_scr[rows, :] * h_prev
        h_scr[rows, :] = hg
        h_prev = jnp.broadcast_to(hg[SUBLANES - 1:SUBLANES, :], hg.shape)
    hc_scr[...] = h_prev

    gr = gr_ref[...].astype(F32)
    o_ref[...] = (h_scr[...] * (gr * jax.nn.sigmoid(gr))).astype(BF16)


def _rglru(proj, conv_w, conv_b, w_gate, b_a, b_x, lru_lambda, bsz, seq):
    n_t = seq // TS
    xr_group, gr_group = 4, 5
    full = lambda shape: pl.BlockSpec(shape, lambda b, t: (0,) * len(shape))
    return pl.pallas_call(
        _rglru_kernel,
        grid=(bsz, n_t),
        in_specs=[
            pl.BlockSpec((TS, REC_WIDTH), lambda b, t: (b * n_t + t, xr_group)),
            pl.BlockSpec((TS, REC_WIDTH), lambda b, t: (b * n_t + t, gr_group)),
            full((CONV_WIDTH, REC_WIDTH)),
            full((1, REC_WIDTH)),
            full((N_REC_BLOCKS, REC_BLOCK, 2 * REC_BLOCK)),
            full((1, REC_WIDTH)), full((1, REC_WIDTH)), full((1, REC_WIDTH)),
        ],
        out_specs=pl.BlockSpec((TS, REC_WIDTH), lambda b, t: (b * n_t + t, 0)),
        out_shape=jax.ShapeDtypeStruct((bsz * seq, REC_WIDTH), BF16),
        scratch_shapes=[
            pltpu.VMEM((TS + SUBLANES, REC_WIDTH), F32),
            pltpu.VMEM((TS, REC_WIDTH), F32),
            pltpu.VMEM((TS, REC_WIDTH), F32),
            pltpu.VMEM((TS, REC_WIDTH), F32),
            pltpu.VMEM((SUBLANES, REC_WIDTH), F32),
        ],
        compiler_params=pltpu.CompilerParams(
            dimension_semantics=("arbitrary", "arbitrary"),
            vmem_limit_bytes=VMEM_LIMIT),
        name="conv_rglru",
    )(proj, proj, conv_w, conv_b, w_gate, b_a, b_x, lru_lambda)


def _outproj_kernel(x_ref, o_ref, r_ref, w_ref, gain_ref, y_ref):
    y = x_ref[...]
    y = y + jnp.dot(o_ref[...], w_ref[0:ATTN_WIDTH, :], preferred_element_type=F32)
    y = y + jnp.dot(r_ref[...], w_ref[ATTN_WIDTH:, :], preferred_element_type=F32)
    ms = jnp.mean(y * y, axis=-1, keepdims=True)
    y_ref[...] = y * lax.rsqrt(ms + NORM_EPS) * gain_ref[...]


def _outproj(x2, o, r, w_out_bf16, final_gain):
    m = x2.shape[0]
    return pl.pallas_call(
        _outproj_kernel,
        grid=(m // TM_OUT,),
        in_specs=[
            pl.BlockSpec((TM_OUT, D_MODEL), lambda i: (i, 0)),
            pl.BlockSpec((TM_OUT, ATTN_WIDTH), lambda i: (i, 0)),
            pl.BlockSpec((TM_OUT, REC_WIDTH), lambda i: (i, 0)),
            pl.BlockSpec((ATTN_WIDTH + REC_WIDTH, D_MODEL), lambda i: (0, 0)),
            pl.BlockSpec((1, D_MODEL), lambda i: (0, 0)),
        ],
        out_specs=pl.BlockSpec((TM_OUT, D_MODEL), lambda i: (i, 0)),
        out_shape=jax.ShapeDtypeStruct((m, D_MODEL), F32),
        compiler_params=pltpu.CompilerParams(
            dimension_semantics=("arbitrary",),
            vmem_limit_bytes=VMEM_LIMIT),
        name="outproj_norm",
    )(x2, o, r, w_out_bf16, final_gain)


def kernel(x, positions, norm_gain, w_in, lambda_q1, lambda_k1, lambda_q2, lambda_k2,
           subln_gain, conv_w, conv_b, w_a, b_a, w_x, b_x, lru_lambda, w_out, final_gain):
    bsz, seq, _ = x.shape
    assert x.shape[2] == D_MODEL and norm_gain.shape[0] == 1
    assert seq % max(TQ, TS) == 0 and (bsz * seq) % max(TM_IN, TM_OUT) == 0
    x2 = x.reshape(bsz * seq, D_MODEL)
    pos2 = positions.reshape(bsz * seq, 1)
    inv_freq = ROPE_THETA ** (-jnp.arange(0, QK_DIM, 2, dtype=F32) / QK_DIM)
    invf = jnp.tile(inv_freq, LANES // ROT_HALF).reshape(1, LANES)

    proj = _inproj(x2, pos2, invf, norm_gain[0].reshape(1, D_MODEL),
                   w_in[0])
    o = _attention(proj, lambda_q1[0].reshape(1, QK_DIM), lambda_k1[0].reshape(1, QK_DIM),
                   lambda_q2[0].reshape(1, QK_DIM), lambda_k2[0].reshape(1, QK_DIM),
                   subln_gain[0].reshape(1, HEAD_DIM), bsz, seq)
    w_gate = jnp.concatenate([w_a[0], w_x[0]], axis=-1).astype(BF16)
    r = _rglru(proj, conv_w[0], conv_b[0].reshape(1, REC_WIDTH), w_gate,
               b_a[0].reshape(1, REC_WIDTH), b_x[0].reshape(1, REC_WIDTH),
               lru_lambda[0].reshape(1, REC_WIDTH), bsz, seq)
    y = _outproj(x2, o, r, w_out[0].astype(BF16), final_gain.reshape(1, D_MODEL))
    return y.reshape(bsz, seq, D_MODEL)
```

```python
import math

import jax
import jax.numpy as jnp
from jax import lax
from jax.experimental import pallas as pl
from jax.experimental.pallas import tpu as pltpu

F32 = jnp.float32
BF16 = jnp.bfloat16

D_MODEL = 2048
ATTN_WIDTH = D_MODEL // 2
REC_WIDTH = D_MODEL // 2
N_ATTN_HEADS = 8
HEAD_DIM = ATTN_WIDTH // N_ATTN_HEADS
QK_DIM = HEAD_DIM // 2
ROT_HALF = QK_DIM // 2
N_REC_BLOCKS = 8
REC_BLOCK = REC_WIDTH // N_REC_BLOCKS
CONV_WIDTH = 4
LRU_C = 8.0
ROPE_THETA = 10000.0
NORM_EPS = 1e-6
IN_WIDTH = 4 * ATTN_WIDTH + 2 * REC_WIDTH
LAM_INIT = 0.8 - 0.6 * math.exp(-0.3 * 0)
QK_SCALE = QK_DIM ** -0.5
LOG2E = math.log2(math.e)

SUBLANES = 8
LANES = 128

TM_IN = 512
TN_IN = 1024
TQ = 256
SOFTMAX_ROWS = 64
TS = 256
TM_OUT = 512
VMEM_LIMIT = 48 * 1024 * 1024
VMEM_LIMIT_INPROJ = 58 * 1024 * 1024


N_COL_TILES = IN_WIDTH // TN_IN
N_ROPE_TILES = 2 * ATTN_WIDTH // TN_IN
N_PLAIN_TILES = N_COL_TILES - N_ROPE_TILES


def _col_tile(j):
    return (j + N_ROPE_TILES) % N_COL_TILES


def _rope_partner(blk, lane):
    fwd = pltpu.roll(blk, LANES - ROT_HALF, axis=1)
    bwd = pltpu.roll(blk, ROT_HALF, axis=1)
    return jnp.where((lane % QK_DIM) < ROT_HALF, fwd, bwd)


def _inproj_kernel(x_ref, pos_ref, invf_ref, gain_ref, w_ref, o_ref,
                   w_scr, h_scr, rstd_scr, cos_scr, sin_scr):
    i = pl.program_id(0)
    j = pl.program_id(1)
    tm = x_ref.shape[0]

    @pl.when(i == 0)
    def _():
        w_scr[j] = w_ref[...].astype(BF16)

    def plain_store(h, rstd):
        rstd2 = jnp.concatenate([rstd, rstd], axis=1)
        for pair in range(TN_IN // (2 * LANES)):
            cs = slice(pair * 2 * LANES, (pair + 1) * 2 * LANES)
            acc = jnp.dot(h, w_scr[j, :, cs], preferred_element_type=F32)
            o_ref[:, cs] = (acc * rstd2).astype(BF16)

    @pl.when(j == 0)
    def _():
        x = x_ref[...]
        h = (x * gain_ref[...]).astype(BF16)
        h_scr[...] = h
        rstd = lax.rsqrt(jnp.mean(x * x, axis=-1, keepdims=True) + NORM_EPS)
        rstd = jnp.broadcast_to(rstd, (tm, LANES))
        rstd_scr[...] = rstd
        plain_store(h, rstd)
        ang = pos_ref[...].astype(F32) * invf_ref[...]
        lane = lax.broadcasted_iota(jnp.int32, (tm, LANES), 1)
        sign = jnp.where((lane % QK_DIM) < ROT_HALF, -1.0, 1.0).astype(F32)
        cos_scr[...] = jnp.cos(ang)
        sin_scr[...] = jnp.sin(ang) * sign

    @pl.when(jnp.logical_and(j > 0, j < N_PLAIN_TILES))
    def _():
        plain_store(h_scr[...], rstd_scr[...])

    @pl.when(j >= N_PLAIN_TILES)
    def _():
        scale = jnp.where(j == N_PLAIN_TILES, QK_SCALE * LOG2E, 1.0).astype(F32)
        rstd = rstd_scr[...]
        cosf = cos_scr[...] * scale
        sinf = sin_scr[...] * scale
        lane = lax.broadcasted_iota(jnp.int32, (tm, LANES), 1)
        for pair in range(TN_IN // (2 * LANES)):
            c0 = pair * 2 * LANES
            acc = jnp.dot(h_scr[...], w_scr[j, :, c0:c0 + 2 * LANES],
                          preferred_element_type=F32)
            for half in range(2):
                blk = acc[:, half * LANES:(half + 1) * LANES] * rstd
                rot = blk * cosf + _rope_partner(blk, lane) * sinf
                o_ref[:, c0 + half * LANES:c0 + (half + 1) * LANES] = rot.astype(BF16)


def _inproj(x2, pos2, invf, gain, w_in):
    m = x2.shape[0]
    grid = (m // TM_IN, N_COL_TILES)
    last_tile = _col_tile(N_COL_TILES - 1)
    return pl.pallas_call(
        _inproj_kernel,
        grid=grid,
        in_specs=[
            pl.BlockSpec((TM_IN, D_MODEL), lambda i, j: (i, 0)),
            pl.BlockSpec((TM_IN, 1), lambda i, j: (i, 0)),
            pl.BlockSpec((1, LANES), lambda i, j: (0, 0)),
            pl.BlockSpec((1, D_MODEL), lambda i, j: (0, 0)),
            pl.BlockSpec((D_MODEL, TN_IN),
                         lambda i, j: (0, jnp.where(i == 0, _col_tile(j), last_tile))),
        ],
        out_specs=pl.BlockSpec((TM_IN, TN_IN), lambda i, j: (i, _col_tile(j))),
        out_shape=jax.ShapeDtypeStruct((m, IN_WIDTH), BF16),
        scratch_shapes=[
            pltpu.VMEM((N_COL_TILES, D_MODEL, TN_IN), BF16),
            pltpu.VMEM((TM_IN, D_MODEL), BF16),
            pltpu.VMEM((TM_IN, LANES), F32),
            pltpu.VMEM((TM_IN, LANES), F32),
            pltpu.VMEM((TM_IN, LANES), F32),
        ],
        compiler_params=pltpu.CompilerParams(
            dimension_semantics=("arbitrary", "arbitrary"),
            vmem_limit_bytes=VMEM_LIMIT_INPROJ),
        name="inproj_rope",
    )(x2, pos2, invf, gain, w_in)


def _dot_nt(a, b):
    return lax.dot_general(a, b, (((1,), (1,)), ((), ())), preferred_element_type=F32)


def _attn_kernel(q_ref, k_ref, v_ref, g_ref, lq1_ref, lk1_ref, lq2_ref, lk2_ref,
                 sg_ref, o_ref, s_scr, p_scr):
    seq = q_ref.shape[0]
    lam = (jnp.exp(jnp.sum(lq1_ref[...] * lk1_ref[...], axis=-1, keepdims=True))
           - jnp.exp(jnp.sum(lq2_ref[...] * lk2_ref[...], axis=-1, keepdims=True))
           + LAM_INIT)
    lane = lax.broadcasted_iota(jnp.int32, (TQ, LANES), 1)
    row = lax.broadcasted_iota(jnp.int32, (2 * TQ, TQ), 0)
    col = lax.broadcasted_iota(jnp.int32, (2 * TQ, TQ), 1)
    causal = col <= jnp.where(row >= TQ, row - TQ, row)

    for i in range(seq // TQ):
        buf = i % 2
        w0, w = i * TQ, (i + 1) * TQ
        q = q_ref[w0:w, :]
        zero = jnp.zeros_like(q)
        qcat = jnp.concatenate([jnp.where(lane < QK_DIM, q, zero),
                                jnp.where(lane < QK_DIM, zero, q)], axis=0)
        if i > 0:
            s_scr[buf, :, 0:w0] = _dot_nt(qcat, k_ref[0:w0, :])
        s_scr[buf, :, w0:w] = jnp.where(causal, _dot_nt(qcat, k_ref[w0:w, :]), -jnp.inf)

        l_parts = []
        for rb in range(2 * TQ // SOFTMAX_ROWS):
            rows = slice(rb * SOFTMAX_ROWS, (rb + 1) * SOFTMAX_ROWS)
            m = jnp.max(s_scr[buf, rows, 0:w], axis=1, keepdims=True)
            p = jnp.exp2(s_scr[buf, rows, 0:w] - m)
            l_parts.append(jnp.sum(p, axis=1, keepdims=True))
            p_scr[buf, rows, 0:w] = p.astype(BF16)
        inv = 1.0 / jnp.concatenate(l_parts, axis=0)

        acc = jnp.dot(p_scr[buf, :, 0:w], v_ref[0:w, :], preferred_element_type=F32)
        o = acc[:TQ] * inv[:TQ] - acc[TQ:] * (inv[TQ:] * lam)
        ms = jnp.mean(o * o, axis=-1, keepdims=True)
        o = o * lax.rsqrt(ms + NORM_EPS) * sg_ref[...] * (1.0 - LAM_INIT)
        g = g_ref[w0:w, :].astype(F32)
        o_ref[w0:w, :] = (o * (g * jax.nn.sigmoid(g))).astype(BF16)


def _attention(proj, lq1, lk1, lq2, lk2, subln, bsz, seq):
    head_block = (seq, HEAD_DIM)
    q_group, k_group, v_group, g_group = (g * N_ATTN_HEADS for g in range(4))
    small = lambda w: pl.BlockSpec((1, w), lambda b, h: (0, 0))
    return pl.pallas_call(
        _attn_kernel,
        grid=(bsz, N_ATTN_HEADS),
        in_specs=[
            pl.BlockSpec(head_block, lambda b, h: (b, q_group + h)),
            pl.BlockSpec(head_block, lambda b, h: (b, k_group + h)),
            pl.BlockSpec(head_block, lambda b, h: (b, v_group + h)),
            pl.BlockSpec(head_block, lambda b, h: (b, g_group + h)),
            small(QK_DIM), small(QK_DIM), small(QK_DIM), small(QK_DIM),
            small(HEAD_DIM),
        ],
        out_specs=pl.BlockSpec(head_block, lambda b, h: (b, h)),
        out_shape=jax.ShapeDtypeStruct((bsz * seq, ATTN_WIDTH), BF16),
        scratch_shapes=[pltpu.VMEM((2, 2 * TQ, seq), F32),
                        pltpu.VMEM((2, 2 * TQ, seq), BF16)],
        compiler_params=pltpu.CompilerParams(
            dimension_semantics=("arbitrary", "arbitrary"),
            vmem_limit_bytes=VMEM_LIMIT),
        name="diff_attention",
    )(proj, proj, proj, proj, lq1, lk1, lq2, lk2, subln)


def _rglru_kernel(xr_ref, gr_ref, cw_ref, cb_ref, wg_ref, ba_ref, bx_ref, lam_ref,
                  o_ref, xe_scr, a_scr, u_scr, h_scr, hc_scr):
    t = pl.program_id(1)
    halo = SUBLANES

    @pl.when(t == 0)
    def _():
        xe_scr[0:halo, :] = jnp.zeros((halo, REC_WIDTH), F32)
        hc_scr[...] = jnp.zeros_like(hc_scr)

    @pl.when(t > 0)
    def _():
        xe_scr[0:halo, :] = xe_scr[TS:TS + halo, :]

    xe_scr[halo:halo + TS, :] = xr_ref[...].astype(F32)

    groups = TS // SUBLANES
    sub = lax.broadcasted_iota(jnp.int32, (groups, SUBLANES, REC_BLOCK), 1)
    for n in range(N_REC_BLOCKS):
        cs = slice(n * REC_BLOCK, (n + 1) * REC_BLOCK)
        y = cb_ref[:, cs]
        for tap in range(CONV_WIDTH):
            shift = CONV_WIDTH - 1 - tap
            y = y + cw_ref[tap:tap + 1, cs] * xe_scr[halo - shift:halo - shift + TS, cs]
        gates = jnp.dot(y.astype(BF16), wg_ref[n], preferred_element_type=F32)
        r = jax.nn.sigmoid(gates[:, :REC_BLOCK] + ba_ref[:, cs])
        ig = jax.nn.sigmoid(gates[:, REC_BLOCK:] + bx_ref[:, cs])
        log_a = r * (-LRU_C * jax.nn.softplus(-lam_ref[:, cs]))
        a = jnp.exp(log_a)
        mult = jnp.sqrt(-jnp.tanh(log_a) * (a * a + 1.0))
        u = mult * (ig * y)
        a3 = a.reshape(groups, SUBLANES, REC_BLOCK)
        u3 = u.reshape(groups, SUBLANES, REC_BLOCK)
        d = 1
        while d < SUBLANES:
            keep = sub >= d
            a_prev = pltpu.roll(a3, d, axis=1)
            u_prev = pltpu.roll(u3, d, axis=1)
            u3 = jnp.where(keep, a3 * u_prev, 0.0) + u3
            a3 = jnp.where(keep, a3 * a_prev, a3)
            d *= 2
        a_scr[:, cs] = a3.reshape(TS, REC_BLOCK)
        u_scr[:, cs] = u3.reshape(TS, REC_BLOCK)

    h_prev = hc_scr[...]
    for g in range(groups):
        rows = slice(g * SUBLANES, (g + 1) * SUBLANES)
        hg = u_scr[rows, :] + a_scr[rows, :] * h_prev
        h_scr[rows, :] = hg
        h_prev = jnp.broadcast_to(hg[SUBLANES - 1:SUBLANES, :], hg.shape)
    hc_scr[...] = h_prev

    gr = gr_ref[...].astype(F32)
    o_ref[...] = (h_scr[...] * (gr * jax.nn.sigmoid(gr))).astype(BF16)


def _rglru(proj, conv_w, conv_b, w_gate, b_a, b_x, lru_lambda, bsz, seq):
    n_t = seq // TS
    xr_group, gr_group = 4, 5
    full = lambda shape: pl.BlockSpec(shape, lambda b, t: (0,) * len(shape))
    return pl.pallas_call(
        _rglru_kernel,
        grid=(bsz, n_t),
        in_specs=[
            pl.BlockSpec((TS, REC_WIDTH), lambda b, t: (b * n_t + t, xr_group)),
            pl.BlockSpec((TS, REC_WIDTH), lambda b, t: (b * n_t + t, gr_group)),
            full((CONV_WIDTH, REC_WIDTH)),
            full((1, REC_WIDTH)),
            full((N_REC_BLOCKS, REC_BLOCK, 2 * REC_BLOCK)),
            full((1, REC_WIDTH)), full((1, REC_WIDTH)), full((1, REC_WIDTH)),
        ],
        out_specs=pl.BlockSpec((TS, REC_WIDTH), lambda b, t: (b * n_t + t, 0)),
        out_shape=jax.ShapeDtypeStruct((bsz * seq, REC_WIDTH), BF16),
        scratch_shapes=[
            pltpu.VMEM((TS + SUBLANES, REC_WIDTH), F32),
            pltpu.VMEM((TS, REC_WIDTH), F32),
            pltpu.VMEM((TS, REC_WIDTH), F32),
            pltpu.VMEM((TS, REC_WIDTH), F32),
            pltpu.VMEM((SUBLANES, REC_WIDTH), F32),
        ],
        compiler_params=pltpu.CompilerParams(
            dimension_semantics=("arbitrary", "arbitrary"),
            vmem_limit_bytes=VMEM_LIMIT),
        name="conv_rglru",
    )(proj, proj, conv_w, conv_b, w_gate, b_a, b_x, lru_lambda)


def _outproj_kernel(x_ref, o_ref, r_ref, w_ref, gain_ref, y_ref, w_scr):
    @pl.when(pl.program_id(0) == 0)
    def _():
        w_scr[...] = w_ref[...].astype(BF16)

    y = x_ref[...]
    y = y + jnp.dot(o_ref[...], w_scr[0:ATTN_WIDTH, :], preferred_element_type=F32)
    y = y + jnp.dot(r_ref[...], w_scr[ATTN_WIDTH:, :], preferred_element_type=F32)
    ms = jnp.mean(y * y, axis=-1, keepdims=True)
    y_ref[...] = y * lax.rsqrt(ms + NORM_EPS) * gain_ref[...]


def _outproj(x2, o, r, w_out, final_gain):
    m = x2.shape[0]
    return pl.pallas_call(
        _outproj_kernel,
        grid=(m // TM_OUT,),
        in_specs=[
            pl.BlockSpec((TM_OUT, D_MODEL), lambda i: (i, 0)),
            pl.BlockSpec((TM_OUT, ATTN_WIDTH), lambda i: (i, 0)),
            pl.BlockSpec((TM_OUT, REC_WIDTH), lambda i: (i, 0)),
            pl.BlockSpec((ATTN_WIDTH + REC_WIDTH, D_MODEL), lambda i: (0, 0)),
            pl.BlockSpec((1, D_MODEL), lambda i: (0, 0)),
        ],
        out_specs=pl.BlockSpec((TM_OUT, D_MODEL), lambda i: (i, 0)),
        out_shape=jax.ShapeDtypeStruct((m, D_MODEL), F32),
        scratch_shapes=[pltpu.VMEM((ATTN_WIDTH + REC_WIDTH, D_MODEL), BF16)],
        compiler_params=pltpu.CompilerParams(
            dimension_semantics=("arbitrary",),
            vmem_limit_bytes=VMEM_LIMIT),
        name="outproj_norm",
    )(x2, o, r, w_out, final_gain)


def kernel(x, positions, norm_gain, w_in, lambda_q1, lambda_k1, lambda_q2, lambda_k2,
           subln_gain, conv_w, conv_b, w_a, b_a, w_x, b_x, lru_lambda, w_out, final_gain):
    bsz, seq, _ = x.shape
    assert x.shape[2] == D_MODEL and norm_gain.shape[0] == 1
    assert seq % max(TQ, TS) == 0 and (bsz * seq) % max(TM_IN, TM_OUT) == 0
    x2 = x.reshape(bsz * seq, D_MODEL)
    pos2 = positions.reshape(bsz * seq, 1)
    inv_freq = ROPE_THETA ** (-jnp.arange(0, QK_DIM, 2, dtype=F32) / QK_DIM)
    invf = jnp.tile(inv_freq, LANES // ROT_HALF).reshape(1, LANES)

    proj = _inproj(x2, pos2, invf, norm_gain[0].reshape(1, D_MODEL),
                   w_in[0])
    o = _attention(proj, lambda_q1[0].reshape(1, QK_DIM), lambda_k1[0].reshape(1, QK_DIM),
                   lambda_q2[0].reshape(1, QK_DIM), lambda_k2[0].reshape(1, QK_DIM),
                   subln_gain[0].reshape(1, HEAD_DIM), bsz, seq)
    w_gate = jnp.concatenate([w_a[0], w_x[0]], axis=-1).astype(BF16)
    r = _rglru(proj, conv_w[0], conv_b[0].reshape(1, REC_WIDTH), w_gate,
               b_a[0].reshape(1, REC_WIDTH), b_x[0].reshape(1, REC_WIDTH),
               lru_lambda[0].reshape(1, REC_WIDTH), bsz, seq)
    y = _outproj(x2, o, r, w_out[0], final_gain.reshape(1, D_MODEL))
    return y.reshape(bsz, seq, D_MODEL)
```

```python
import math

import jax
import jax.numpy as jnp
from jax import lax
from jax.experimental import pallas as pl
from jax.experimental.pallas import tpu as pltpu

F32 = jnp.float32
BF16 = jnp.bfloat16

D_MODEL = 2048
ATTN_WIDTH = D_MODEL // 2
REC_WIDTH = D_MODEL // 2
N_ATTN_HEADS = 8
HEAD_DIM = ATTN_WIDTH // N_ATTN_HEADS
QK_DIM = HEAD_DIM // 2
ROT_HALF = QK_DIM // 2
N_REC_BLOCKS = 8
REC_BLOCK = REC_WIDTH // N_REC_BLOCKS
CONV_WIDTH = 4
LRU_C = 8.0
ROPE_THETA = 10000.0
NORM_EPS = 1e-6
IN_WIDTH = 4 * ATTN_WIDTH + 2 * REC_WIDTH
LAM_INIT = 0.8 - 0.6 * math.exp(-0.3 * 0)
QK_SCALE = QK_DIM ** -0.5
LOG2E = math.log2(math.e)

SUBLANES = 8
LANES = 128

TM_IN = 512
TN_IN = 1024
TQ = 256
SOFTMAX_ROWS = 64
TS = 256
TM_OUT = 512
VMEM_LIMIT = 48 * 1024 * 1024
VMEM_LIMIT_INPROJ = 58 * 1024 * 1024


def _sigmoid(x):
    return 0.5 * jnp.tanh(0.5 * x) + 0.5


N_COL_TILES = IN_WIDTH // TN_IN
N_ROPE_TILES = 2 * ATTN_WIDTH // TN_IN
N_PLAIN_TILES = N_COL_TILES - N_ROPE_TILES


def _col_tile(j):
    return (j + N_ROPE_TILES) % N_COL_TILES


def _rope_partner(blk, lane):
    fwd = pltpu.roll(blk, LANES - ROT_HALF, axis=1)
    bwd = pltpu.roll(blk, ROT_HALF, axis=1)
    return jnp.where((lane % QK_DIM) < ROT_HALF, fwd, bwd)


def _inproj_kernel(x_ref, pos_ref, invf_ref, gain_ref, w_ref, o_ref,
                   w_scr, h_scr, rstd_scr, cos_scr, sin_scr):
    i = pl.program_id(0)
    j = pl.program_id(1)
    tm = x_ref.shape[0]

    @pl.when(i == 0)
    def _():
        w_scr[j] = w_ref[...].astype(BF16)

    def plain_store(h, rstd):
        rstd2 = jnp.concatenate([rstd, rstd], axis=1)
        for pair in range(TN_IN // (2 * LANES)):
            cs = slice(pair * 2 * LANES, (pair + 1) * 2 * LANES)
            acc = jnp.dot(h, w_scr[j, :, cs], preferred_element_type=F32)
            o_ref[:, cs] = (acc * rstd2).astype(BF16)

    @pl.when(j == 0)
    def _():
        x = x_ref[...]
        h = (x * gain_ref[...]).astype(BF16)
        h_scr[...] = h
        rstd = lax.rsqrt(jnp.mean(x * x, axis=-1, keepdims=True) + NORM_EPS)
        rstd = jnp.broadcast_to(rstd, (tm, LANES))
        rstd_scr[...] = rstd
        plain_store(h, rstd)
        ang = pos_ref[...].astype(F32) * invf_ref[...]
        lane = lax.broadcasted_iota(jnp.int32, (tm, LANES), 1)
        sign = jnp.where((lane % QK_DIM) < ROT_HALF, -1.0, 1.0).astype(F32)
        cos_scr[...] = jnp.cos(ang)
        sin_scr[...] = jnp.sin(ang) * sign

    @pl.when(jnp.logical_and(j > 0, j < N_PLAIN_TILES))
    def _():
        plain_store(h_scr[...], rstd_scr[...])

    @pl.when(j >= N_PLAIN_TILES)
    def _():
        scale = jnp.where(j == N_PLAIN_TILES, QK_SCALE * LOG2E, 1.0).astype(F32)
        rstd = rstd_scr[...]
        cosf = cos_scr[...] * scale
        sinf = sin_scr[...] * scale
        lane = lax.broadcasted_iota(jnp.int32, (tm, LANES), 1)
        for pair in range(TN_IN // (2 * LANES)):
            c0 = pair * 2 * LANES
            acc = jnp.dot(h_scr[...], w_scr[j, :, c0:c0 + 2 * LANES],
                          preferred_element_type=F32)
            for half in range(2):
                blk = acc[:, half * LANES:(half + 1) * LANES] * rstd
                rot = blk * cosf + _rope_partner(blk, lane) * sinf
                o_ref[:, c0 + half * LANES:c0 + (half + 1) * LANES] = rot.astype(BF16)


def _inproj(x2, pos2, invf, gain, w_in):
    m = x2.shape[0]
    grid = (m // TM_IN, N_COL_TILES)
    last_tile = _col_tile(N_COL_TILES - 1)
    return pl.pallas_call(
        _inproj_kernel,
        grid=grid,
        in_specs=[
            pl.BlockSpec((TM_IN, D_MODEL), lambda i, j: (i, 0)),
            pl.BlockSpec((TM_IN, 1), lambda i, j: (i, 0)),
            pl.BlockSpec((1, LANES), lambda i, j: (0, 0)),
            pl.BlockSpec((1, D_MODEL), lambda i, j: (0, 0)),
            pl.BlockSpec((D_MODEL, TN_IN),
                         lambda i, j: (0, jnp.where(i == 0, _col_tile(j), last_tile))),
        ],
        out_specs=pl.BlockSpec((TM_IN, TN_IN), lambda i, j: (i, _col_tile(j))),
        out_shape=jax.ShapeDtypeStruct((m, IN_WIDTH), BF16),
        scratch_shapes=[
            pltpu.VMEM((N_COL_TILES, D_MODEL, TN_IN), BF16),
            pltpu.VMEM((TM_IN, D_MODEL), BF16),
            pltpu.VMEM((TM_IN, LANES), F32),
            pltpu.VMEM((TM_IN, LANES), F32),
            pltpu.VMEM((TM_IN, LANES), F32),
        ],
        compiler_params=pltpu.CompilerParams(
            dimension_semantics=("arbitrary", "arbitrary"),
            vmem_limit_bytes=VMEM_LIMIT_INPROJ),
        name="inproj_rope",
    )(x2, pos2, invf, gain, w_in)


def _dot_nt(a, b):
    return lax.dot_general(a, b, (((1,), (1,)), ((), ())), preferred_element_type=F32)


def _attn_kernel(q_ref, k_ref, v_ref, g_ref, lq1_ref, lk1_ref, lq2_ref, lk2_ref,
                 sg_ref, o_ref, s_scr, p_scr):
    seq = q_ref.shape[0]
    lam = (jnp.exp(jnp.sum(lq1_ref[...] * lk1_ref[...], axis=-1, keepdims=True))
           - jnp.exp(jnp.sum(lq2_ref[...] * lk2_ref[...], axis=-1, keepdims=True))
           + LAM_INIT)
    lane = lax.broadcasted_iota(jnp.int32, (TQ, LANES), 1)
    row = lax.broadcasted_iota(jnp.int32, (2 * TQ, TQ), 0)
    col = lax.broadcasted_iota(jnp.int32, (2 * TQ, TQ), 1)
    causal = col <= jnp.where(row >= TQ, row - TQ, row)

    for i in range(seq // TQ):
        buf = i % 2
        w0, w = i * TQ, (i + 1) * TQ
        q = q_ref[w0:w, :]
        zero = jnp.zeros_like(q)
        qcat = jnp.concatenate([jnp.where(lane < QK_DIM, q, zero),
                                jnp.where(lane < QK_DIM, zero, q)], axis=0)
        if i > 0:
            s_scr[buf, :, 0:w0] = _dot_nt(qcat, k_ref[0:w0, :])
        s_scr[buf, :, w0:w] = jnp.where(causal, _dot_nt(qcat, k_ref[w0:w, :]), -jnp.inf)

        l_parts = []
        for rb in range(2 * TQ // SOFTMAX_ROWS):
            rows = slice(rb * SOFTMAX_ROWS, (rb + 1) * SOFTMAX_ROWS)
            m = jnp.max(s_scr[buf, rows, 0:w], axis=1, keepdims=True)
            p = jnp.exp2(s_scr[buf, rows, 0:w] - m)
            l_parts.append(jnp.sum(p, axis=1, keepdims=True))
            p_scr[buf, rows, 0:w] = p.astype(BF16)
        inv = 1.0 / jnp.concatenate(l_parts, axis=0)

        acc = jnp.dot(p_scr[buf, :, 0:w], v_ref[0:w, :], preferred_element_type=F32)
        o = acc[:TQ] * inv[:TQ] - acc[TQ:] * (inv[TQ:] * lam)
        ms = jnp.mean(o * o, axis=-1, keepdims=True)
        o = o * lax.rsqrt(ms + NORM_EPS) * sg_ref[...] * (1.0 - LAM_INIT)
        g = g_ref[w0:w, :].astype(F32)
        o_ref[w0:w, :] = (o * (g * _sigmoid(g))).astype(BF16)


def _attention(proj, lq1, lk1, lq2, lk2, subln, bsz, seq):
    head_block = (seq, HEAD_DIM)
    q_group, k_group, v_group, g_group = (g * N_ATTN_HEADS for g in range(4))
    small = lambda w: pl.BlockSpec((1, w), lambda b, h: (0, 0))
    return pl.pallas_call(
        _attn_kernel,
        grid=(bsz, N_ATTN_HEADS),
        in_specs=[
            pl.BlockSpec(head_block, lambda b, h: (b, q_group + h)),
            pl.BlockSpec(head_block, lambda b, h: (b, k_group + h)),
            pl.BlockSpec(head_block, lambda b, h: (b, v_group + h)),
            pl.BlockSpec(head_block, lambda b, h: (b, g_group + h)),
            small(QK_DIM), small(QK_DIM), small(QK_DIM), small(QK_DIM),
            small(HEAD_DIM),
        ],
        out_specs=pl.BlockSpec(head_block, lambda b, h: (b, h)),
        out_shape=jax.ShapeDtypeStruct((bsz * seq, ATTN_WIDTH), BF16),
        scratch_shapes=[pltpu.VMEM((2, 2 * TQ, seq), F32),
                        pltpu.VMEM((2, 2 * TQ, seq), BF16)],
        compiler_params=pltpu.CompilerParams(
            dimension_semantics=("arbitrary", "arbitrary"),
            vmem_limit_bytes=VMEM_LIMIT),
        name="diff_attention",
    )(proj, proj, proj, proj, lq1, lk1, lq2, lk2, subln)


SEG = TS // SUBLANES
SEG_PITCH = SEG + SUBLANES


def _sublane_scan(p, q, sub):
    d = 1
    while d < SUBLANES:
        keep = sub >= d
        p_prev = pltpu.roll(p, d, axis=0)
        q_prev = pltpu.roll(q, d, axis=0)
        q = jnp.where(keep, p * q_prev, 0.0) + q
        p = jnp.where(keep, p * p_prev, p)
        d *= 2
    return p, q


def _rglru_kernel(xr_ref, gr_ref, cw_ref, cb_ref, wg_ref, ba_ref, bx_ref, lam_ref,
                  o_ref, tail_scr, hc_scr, *scratch):
    x_scr, h_scr, a_scr, u_scr = (scratch[k * N_REC_BLOCKS:(k + 1) * N_REC_BLOCKS]
                                  for k in range(4))
    t = pl.program_id(1)

    @pl.when(t == 0)
    def _():
        tail_scr[...] = jnp.zeros_like(tail_scr)
        hc_scr[...] = jnp.zeros_like(hc_scr)

    sub = lax.broadcasted_iota(jnp.int32, (SUBLANES, REC_BLOCK), 0)
    seg_rows = lambda s: slice(s * SEG_PITCH, s * SEG_PITCH + SEG)
    seg_step = lambda g: pl.ds(g, SUBLANES, stride=SEG_PITCH)
    step_rows = lambda g: slice(g * SUBLANES, (g + 1) * SUBLANES)

    for n in range(N_REC_BLOCKS):
        xf = xr_ref[:, n * REC_BLOCK:(n + 1) * REC_BLOCK].astype(F32)
        for s in range(SUBLANES):
            x_scr[n][seg_rows(s), :] = xf[s * SEG:(s + 1) * SEG]

    for n in range(N_REC_BLOCKS):
        cs = slice(n * REC_BLOCK, (n + 1) * REC_BLOCK)
        xs = [x_scr[n][seg_step(g), :] for g in range(SEG)]
        before = {}
        for j in range(1, CONV_WIDTH):
            cur = pltpu.roll(xs[SEG - j], 1, axis=0)
            before[-j] = jnp.where(sub == 0, tail_scr[j - 1, :, cs], cur)
            tail_scr[j - 1, :, cs] = cur
        x_at = lambda g: xs[g] if g >= 0 else before[g]
        taps = [cw_ref[CONV_WIDTH - 1 - j:CONV_WIDTH - j, cs] for j in range(CONV_WIDTH)]
        blocks = []
        for g in range(SEG):
            yb = cb_ref[:, cs]
            for j in range(CONV_WIDTH):
                yb = yb + taps[j] * x_at(g - j)
            blocks.append(yb)
        y = jnp.concatenate(blocks, axis=0)

        gates = jnp.dot(y.astype(BF16), wg_ref[n], preferred_element_type=F32)
        r = _sigmoid(gates[:, :REC_BLOCK] + ba_ref[:, cs])
        ig = _sigmoid(gates[:, REC_BLOCK:] + bx_ref[:, cs])
        log_a = r * (-LRU_C * jax.nn.softplus(-lam_ref[:, cs]))
        a = jnp.exp(log_a)
        gap = -jnp.tanh(log_a) * (a * a + 1.0)
        mult = jnp.where(gap > 0.0, gap * lax.rsqrt(gap), 0.0)
        a_scr[n][...] = a
        u_scr[n][...] = mult * (ig * y)

    for n in range(N_REC_BLOCKS):
        cs = slice(n * REC_BLOCK, (n + 1) * REC_BLOCK)
        p = jnp.ones((SUBLANES, REC_BLOCK), F32)
        q = jnp.zeros((SUBLANES, REC_BLOCK), F32)
        for g in range(SEG):
            ag = a_scr[n][step_rows(g), :]
            q = ag * q + u_scr[n][step_rows(g), :]
            p = ag * p
        p, q = _sublane_scan(p, q, sub)
        carry = hc_scr[:, cs]
        seg_end = q + p * carry
        h = jnp.where(sub == 0, carry, pltpu.roll(seg_end, 1, axis=0))
        hc_scr[:, cs] = jnp.broadcast_to(seg_end[SUBLANES - 1:SUBLANES], seg_end.shape)
        for g in range(SEG):
            h = a_scr[n][step_rows(g), :] * h + u_scr[n][step_rows(g), :]
            h_scr[n][seg_step(g), :] = h

    for n in range(N_REC_BLOCKS):
        cs = slice(n * REC_BLOCK, (n + 1) * REC_BLOCK)
        h = jnp.concatenate([h_scr[n][seg_rows(s), :] for s in range(SUBLANES)], axis=0)
        gr = gr_ref[:, cs].astype(F32)
        o_ref[:, cs] = (h * (gr * _sigmoid(gr))).astype(BF16)


def _rglru(proj, conv_w, conv_b, w_gate, b_a, b_x, lru_lambda, bsz, seq):
    n_t = seq // TS
    xr_group, gr_group = 4, 5
    full = lambda shape: pl.BlockSpec(shape, lambda b, t: (0,) * len(shape))
    slab = pltpu.VMEM((SUBLANES * SEG_PITCH, REC_BLOCK), F32)
    plain = pltpu.VMEM((TS, REC_BLOCK), F32)
    return pl.pallas_call(
        _rglru_kernel,
        grid=(bsz, n_t),
        in_specs=[
            pl.BlockSpec((TS, REC_WIDTH), lambda b, t: (b * n_t + t, xr_group)),
            pl.BlockSpec((TS, REC_WIDTH), lambda b, t: (b * n_t + t, gr_group)),
            full((CONV_WIDTH, REC_WIDTH)),
            full((1, REC_WIDTH)),
            full((N_REC_BLOCKS, REC_BLOCK, 2 * REC_BLOCK)),
            full((1, REC_WIDTH)), full((1, REC_WIDTH)), full((1, REC_WIDTH)),
        ],
        out_specs=pl.BlockSpec((TS, REC_WIDTH), lambda b, t: (b * n_t + t, 0)),
        out_shape=jax.ShapeDtypeStruct((bsz * seq, REC_WIDTH), BF16),
        scratch_shapes=[
            pltpu.VMEM((CONV_WIDTH - 1, SUBLANES, REC_WIDTH), F32),
            pltpu.VMEM((SUBLANES, REC_WIDTH), F32),
        ] + [slab] * (2 * N_REC_BLOCKS) + [plain] * (2 * N_REC_BLOCKS),
        compiler_params=pltpu.CompilerParams(
            dimension_semantics=("arbitrary", "arbitrary"),
            vmem_limit_bytes=VMEM_LIMIT),
        name="conv_rglru",
    )(proj, proj, conv_w, conv_b, w_gate, b_a, b_x, lru_lambda)


def _outproj_kernel(x_ref, o_ref, r_ref, w_ref, gain_ref, y_ref, w_scr):
    @pl.when(pl.program_id(0) == 0)
    def _():
        w_scr[...] = w_ref[...].astype(BF16)

    y = x_ref[...]
    y = y + jnp.dot(o_ref[...], w_scr[0:ATTN_WIDTH, :], preferred_element_type=F32)
    y = y + jnp.dot(r_ref[...], w_scr[ATTN_WIDTH:, :], preferred_element_type=F32)
    ms = jnp.mean(y * y, axis=-1, keepdims=True)
    y_ref[...] = y * lax.rsqrt(ms + NORM_EPS) * gain_ref[...]


def _outproj(x2, o, r, w_out, final_gain):
    m = x2.shape[0]
    return pl.pallas_call(
        _outproj_kernel,
        grid=(m // TM_OUT,),
        in_specs=[
            pl.BlockSpec((TM_OUT, D_MODEL), lambda i: (i, 0)),
            pl.BlockSpec((TM_OUT, ATTN_WIDTH), lambda i: (i, 0)),
            pl.BlockSpec((TM_OUT, REC_WIDTH), lambda i: (i, 0)),
            pl.BlockSpec((ATTN_WIDTH + REC_WIDTH, D_MODEL), lambda i: (0, 0)),
            pl.BlockSpec((1, D_MODEL), lambda i: (0, 0)),
        ],
        out_specs=pl.BlockSpec((TM_OUT, D_MODEL), lambda i: (i, 0)),
        out_shape=jax.ShapeDtypeStruct((m, D_MODEL), F32),
        scratch_shapes=[pltpu.VMEM((ATTN_WIDTH + REC_WIDTH, D_MODEL), BF16)],
        compiler_params=pltpu.CompilerParams(
            dimension_semantics=("arbitrary",),
            vmem_limit_bytes=VMEM_LIMIT),
        name="outproj_norm",
    )(x2, o, r, w_out, final_gain)


def kernel(x, positions, norm_gain, w_in, lambda_q1, lambda_k1, lambda_q2, lambda_k2,
           subln_gain, conv_w, conv_b, w_a, b_a, w_x, b_x, lru_lambda, w_out, final_gain):
    bsz, seq, _ = x.shape
    assert x.shape[2] == D_MODEL and norm_gain.shape[0] == 1
    assert seq % max(TQ, TS) == 0 and (bsz * seq) % max(TM_IN, TM_OUT) == 0
    x2 = x.reshape(bsz * seq, D_MODEL)
    pos2 = positions.reshape(bsz * seq, 1)
    inv_freq = ROPE_THETA ** (-jnp.arange(0, QK_DIM, 2, dtype=F32) / QK_DIM)
    invf = jnp.tile(inv_freq, LANES // ROT_HALF).reshape(1, LANES)

    proj = _inproj(x2, pos2, invf, norm_gain[0].reshape(1, D_MODEL), w_in[0])
    o = _attention(proj, lambda_q1[0].reshape(1, QK_DIM), lambda_k1[0].reshape(1, QK_DIM),
                   lambda_q2[0].reshape(1, QK_DIM), lambda_k2[0].reshape(1, QK_DIM),
                   subln_gain[0].reshape(1, HEAD_DIM), bsz, seq)
    w_gate = jnp.concatenate([w_a[0], w_x[0]], axis=-1).astype(BF16)
    r = _rglru(proj, conv_w[0], conv_b[0].reshape(1, REC_WIDTH), w_gate,
               b_a[0].reshape(1, REC_WIDTH), b_x[0].reshape(1, REC_WIDTH),
               lru_lambda[0].reshape(1, REC_WIDTH), bsz, seq)
    y = _outproj(x2, o, r, w_out[0], final_gain.reshape(1, D_MODEL))
    return y.reshape(bsz, seq, D_MODEL)
```

```python
import math

import jax
import jax.numpy as jnp
from jax import lax
from jax.experimental import pallas as pl
from jax.experimental.pallas import tpu as pltpu

F32 = jnp.float32
BF16 = jnp.bfloat16

D_MODEL = 2048
ATTN_WIDTH = D_MODEL // 2
REC_WIDTH = D_MODEL // 2
N_ATTN_HEADS = 8
HEAD_DIM = ATTN_WIDTH // N_ATTN_HEADS
QK_DIM = HEAD_DIM // 2
ROT_HALF = QK_DIM // 2
N_REC_BLOCKS = 8
REC_BLOCK = REC_WIDTH // N_REC_BLOCKS
CONV_WIDTH = 4
LRU_C = 8.0
ROPE_THETA = 10000.0
NORM_EPS = 1e-6
IN_WIDTH = 4 * ATTN_WIDTH + 2 * REC_WIDTH
LAM_INIT = 0.8 - 0.6 * math.exp(-0.3 * 0)
QK_SCALE = QK_DIM ** -0.5
LOG2E = math.log2(math.e)

SUBLANES = 8
LANES = 128

TM_IN = 512
POS_ROWS = 8
TN_IN = 1024
TQ = 256
SOFTMAX_ROWS = 64
TS = 256
TM_OUT = 512
VMEM_LIMIT = 48 * 1024 * 1024
VMEM_LIMIT_INPROJ = 58 * 1024 * 1024


def _sigmoid(x):
    return 0.5 * jnp.tanh(0.5 * x) + 0.5


N_COL_TILES = IN_WIDTH // TN_IN
N_ROPE_TILES = 2 * ATTN_WIDTH // TN_IN
N_PLAIN_TILES = N_COL_TILES - N_ROPE_TILES


def _col_tile(j):
    return (j + N_ROPE_TILES) % N_COL_TILES


def _rope_partner(blk, lane):
    fwd = pltpu.roll(blk, LANES - ROT_HALF, axis=1)
    bwd = pltpu.roll(blk, ROT_HALF, axis=1)
    return jnp.where((lane % QK_DIM) < ROT_HALF, fwd, bwd)


def _inproj_kernel(x_ref, pos_ref, invf_ref, gain_ref, w_ref, o_ref,
                   w_scr, h_scr, rstd_scr, cos_scr, sin_scr):
    i = pl.program_id(0)
    j = pl.program_id(1)
    tm = x_ref.shape[0]

    @pl.when(i == 0)
    def _():
        w_scr[j] = w_ref[...].astype(BF16)

    def plain_store(h, rstd):
        rstd2 = jnp.concatenate([rstd, rstd], axis=1)
        for pair in range(TN_IN // (2 * LANES)):
            cs = slice(pair * 2 * LANES, (pair + 1) * 2 * LANES)
            acc = jnp.dot(h, w_scr[j, :, cs], preferred_element_type=F32)
            o_ref[:, cs] = (acc * rstd2).astype(BF16)

    @pl.when(j == 0)
    def _():
        x = x_ref[...]
        h = (x * gain_ref[...]).astype(BF16)
        h_scr[...] = h
        rstd = lax.rsqrt(jnp.mean(x * x, axis=-1, keepdims=True) + NORM_EPS)
        rstd = jnp.broadcast_to(rstd, (tm, LANES))
        rstd_scr[...] = rstd
        plain_store(h, rstd)
        first = (i % (POS_ROWS * LANES // tm)) * (tm // LANES)
        pos = pos_ref[pl.ds(first, tm // LANES), :].astype(F32)
        invf = invf_ref[...]
        ang = jnp.concatenate(
            [jnp.broadcast_to(pos[b:b + 1], (LANES, LANES)).T * invf
             for b in range(tm // LANES)], axis=0)
        lane = lax.broadcasted_iota(jnp.int32, (tm, LANES), 1)
        sign = jnp.where((lane % QK_DIM) < ROT_HALF, -1.0, 1.0).astype(F32)
        cos_scr[...] = jnp.cos(ang)
        sin_scr[...] = jnp.sin(ang) * sign

    @pl.when(jnp.logical_and(j > 0, j < N_PLAIN_TILES))
    def _():
        plain_store(h_scr[...], rstd_scr[...])

    @pl.when(j >= N_PLAIN_TILES)
    def _():
        scale = jnp.where(j == N_PLAIN_TILES, QK_SCALE * LOG2E, 1.0).astype(F32)
        rstd = rstd_scr[...]
        cosf = cos_scr[...] * scale
        sinf = sin_scr[...] * scale
        lane = lax.broadcasted_iota(jnp.int32, (tm, LANES), 1)
        for pair in range(TN_IN // (2 * LANES)):
            c0 = pair * 2 * LANES
            acc = jnp.dot(h_scr[...], w_scr[j, :, c0:c0 + 2 * LANES],
                          preferred_element_type=F32)
            for half in range(2):
                blk = acc[:, half * LANES:(half + 1) * LANES] * rstd
                rot = blk * cosf + _rope_partner(blk, lane) * sinf
                o_ref[:, c0 + half * LANES:c0 + (half + 1) * LANES] = rot.astype(BF16)


def _inproj(x2, pos2, invf, gain, w_in):
    m = x2.shape[0]
    grid = (m // TM_IN, N_COL_TILES)
    last_tile = _col_tile(N_COL_TILES - 1)
    return pl.pallas_call(
        _inproj_kernel,
        grid=grid,
        in_specs=[
            pl.BlockSpec((TM_IN, D_MODEL), lambda i, j: (i, 0)),
            pl.BlockSpec((POS_ROWS, LANES), lambda i, j: (i * TM_IN // (POS_ROWS * LANES), 0)),
            pl.BlockSpec((1, LANES), lambda i, j: (0, 0)),
            pl.BlockSpec((1, D_MODEL), lambda i, j: (0, 0)),
            pl.BlockSpec((D_MODEL, TN_IN),
                         lambda i, j: (0, jnp.where(i == 0, _col_tile(j), last_tile))),
        ],
        out_specs=pl.BlockSpec((TM_IN, TN_IN), lambda i, j: (i, _col_tile(j))),
        out_shape=jax.ShapeDtypeStruct((m, IN_WIDTH), BF16),
        scratch_shapes=[
            pltpu.VMEM((N_COL_TILES, D_MODEL, TN_IN), BF16),
            pltpu.VMEM((TM_IN, D_MODEL), BF16),
            pltpu.VMEM((TM_IN, LANES), F32),
            pltpu.VMEM((TM_IN, LANES), F32),
            pltpu.VMEM((TM_IN, LANES), F32),
        ],
        compiler_params=pltpu.CompilerParams(
            dimension_semantics=("arbitrary", "arbitrary"),
            vmem_limit_bytes=VMEM_LIMIT_INPROJ),
        name="inproj_rope",
    )(x2, pos2, invf, gain, w_in)


def _dot_nt(a, b):
    return lax.dot_general(a, b, (((1,), (1,)), ((), ())), preferred_element_type=F32)


def _attn_kernel(q_ref, k_ref, v_ref, g_ref, lq1_ref, lk1_ref, lq2_ref, lk2_ref,
                 sg_ref, o_ref, s_scr, p_scr):
    seq = q_ref.shape[0]
    lam = (jnp.exp(jnp.sum(lq1_ref[...] * lk1_ref[...], axis=-1, keepdims=True))
           - jnp.exp(jnp.sum(lq2_ref[...] * lk2_ref[...], axis=-1, keepdims=True))
           + LAM_INIT)
    lane = lax.broadcasted_iota(jnp.int32, (TQ, LANES), 1)
    row = lax.broadcasted_iota(jnp.int32, (2 * TQ, TQ), 0)
    col = lax.broadcasted_iota(jnp.int32, (2 * TQ, TQ), 1)
    causal = col <= jnp.where(row >= TQ, row - TQ, row)

    for i in range(seq // TQ):
        buf = i % 2
        w0, w = i * TQ, (i + 1) * TQ
        q = q_ref[w0:w, :]
        zero = jnp.zeros_like(q)
        qcat = jnp.concatenate([jnp.where(lane < QK_DIM, q, zero),
                                jnp.where(lane < QK_DIM, zero, q)], axis=0)
        if i > 0:
            s_scr[buf, :, 0:w0] = _dot_nt(qcat, k_ref[0:w0, :])
        s_scr[buf, :, w0:w] = jnp.where(causal, _dot_nt(qcat, k_ref[w0:w, :]), -jnp.inf)

        l_parts = []
        for rb in range(2 * TQ // SOFTMAX_ROWS):
            rows = slice(rb * SOFTMAX_ROWS, (rb + 1) * SOFTMAX_ROWS)
            m = jnp.max(s_scr[buf, rows, 0:w], axis=1, keepdims=True)
            p = jnp.exp2(s_scr[buf, rows, 0:w] - m)
            l_parts.append(jnp.sum(p, axis=1, keepdims=True))
            p_scr[buf, rows, 0:w] = p.astype(BF16)
        inv = 1.0 / jnp.concatenate(l_parts, axis=0)

        acc = jnp.dot(p_scr[buf, :, 0:w], v_ref[0:w, :], preferred_element_type=F32)
        o = acc[:TQ] * inv[:TQ] - acc[TQ:] * (inv[TQ:] * lam)
        ms = jnp.mean(o * o, axis=-1, keepdims=True)
        o = o * lax.rsqrt(ms + NORM_EPS) * sg_ref[...] * (1.0 - LAM_INIT)
        g = g_ref[w0:w, :].astype(F32)
        o_ref[w0:w, :] = (o * (g * _sigmoid(g))).astype(BF16)


def _attention(proj, lq1, lk1, lq2, lk2, subln, bsz, seq):
    head_block = (seq, HEAD_DIM)
    q_group, k_group, v_group, g_group = (g * N_ATTN_HEADS for g in range(4))
    small = lambda w: pl.BlockSpec((1, w), lambda b, h: (0, 0))
    return pl.pallas_call(
        _attn_kernel,
        grid=(bsz, N_ATTN_HEADS),
        in_specs=[
            pl.BlockSpec(head_block, lambda b, h: (b, q_group + h)),
            pl.BlockSpec(head_block, lambda b, h: (b, k_group + h)),
            pl.BlockSpec(head_block, lambda b, h: (b, v_group + h)),
            pl.BlockSpec(head_block, lambda b, h: (b, g_group + h)),
            small(QK_DIM), small(QK_DIM), small(QK_DIM), small(QK_DIM),
            small(HEAD_DIM),
        ],
        out_specs=pl.BlockSpec(head_block, lambda b, h: (b, h)),
        out_shape=jax.ShapeDtypeStruct((bsz * seq, ATTN_WIDTH), BF16),
        scratch_shapes=[pltpu.VMEM((2, 2 * TQ, seq), F32),
                        pltpu.VMEM((2, 2 * TQ, seq), BF16)],
        compiler_params=pltpu.CompilerParams(
            dimension_semantics=("arbitrary", "arbitrary"),
            vmem_limit_bytes=VMEM_LIMIT),
        name="diff_attention",
    )(proj, proj, proj, proj, lq1, lk1, lq2, lk2, subln)


SEG = TS // SUBLANES
SEG_PITCH = SEG + SUBLANES


def _sublane_scan(p, q, sub):
    d = 1
    while d < SUBLANES:
        keep = sub >= d
        p_prev = pltpu.roll(p, d, axis=0)
        q_prev = pltpu.roll(q, d, axis=0)
        q = jnp.where(keep, p * q_prev, 0.0) + q
        p = jnp.where(keep, p * p_prev, p)
        d *= 2
    return p, q


def _rglru_kernel(xr_ref, gr_ref, cw_ref, cb_ref, wg_ref, ba_ref, bx_ref, lam_ref,
                  o_ref, tail_scr, hc_scr, *scratch):
    x_scr, h_scr, a_scr, u_scr = (scratch[k * N_REC_BLOCKS:(k + 1) * N_REC_BLOCKS]
                                  for k in range(4))
    t = pl.program_id(1)

    @pl.when(t == 0)
    def _():
        tail_scr[...] = jnp.zeros_like(tail_scr)
        hc_scr[...] = jnp.zeros_like(hc_scr)

    sub = lax.broadcasted_iota(jnp.int32, (SUBLANES, REC_BLOCK), 0)
    seg_rows = lambda s: slice(s * SEG_PITCH, s * SEG_PITCH + SEG)
    seg_step = lambda g: pl.ds(g, SUBLANES, stride=SEG_PITCH)
    step_rows = lambda g: slice(g * SUBLANES, (g + 1) * SUBLANES)

    for n in range(N_REC_BLOCKS):
        xf = xr_ref[:, n * REC_BLOCK:(n + 1) * REC_BLOCK].astype(F32)
        for s in range(SUBLANES):
            x_scr[n][seg_rows(s), :] = xf[s * SEG:(s + 1) * SEG]

    for n in range(N_REC_BLOCKS):
        cs = slice(n * REC_BLOCK, (n + 1) * REC_BLOCK)
        xs = [x_scr[n][seg_step(g), :] for g in range(SEG)]
        before = {}
        for j in range(1, CONV_WIDTH):
            cur = pltpu.roll(xs[SEG - j], 1, axis=0)
            before[-j] = jnp.where(sub == 0, tail_scr[j - 1, :, cs], cur)
            tail_scr[j - 1, :, cs] = cur
        x_at = lambda g: xs[g] if g >= 0 else before[g]
        taps = [cw_ref[CONV_WIDTH - 1 - j:CONV_WIDTH - j, cs] for j in range(CONV_WIDTH)]
        blocks = []
        for g in range(SEG):
            yb = cb_ref[:, cs]
            for j in range(CONV_WIDTH):
                yb = yb + taps[j] * x_at(g - j)
            blocks.append(yb)
        y = jnp.concatenate(blocks, axis=0)

        gates = jnp.dot(y.astype(BF16), wg_ref[n], preferred_element_type=F32)
        r = _sigmoid(gates[:, :REC_BLOCK] + ba_ref[:, cs])
        ig = _sigmoid(gates[:, REC_BLOCK:] + bx_ref[:, cs])
        log_a = r * (-LRU_C * jax.nn.softplus(-lam_ref[:, cs]))
        a = jnp.exp(log_a)
        gap = -jnp.tanh(log_a) * (a * a + 1.0)
        mult = jnp.where(gap > 0.0, gap * lax.rsqrt(gap), 0.0)
        a_scr[n][...] = a
        u_scr[n][...] = mult * (ig * y)

    for n in range(N_REC_BLOCKS):
        cs = slice(n * REC_BLOCK, (n + 1) * REC_BLOCK)
        p = jnp.ones((SUBLANES, REC_BLOCK), F32)
        q = jnp.zeros((SUBLANES, REC_BLOCK), F32)
        for g in range(SEG):
            ag = a_scr[n][step_rows(g), :]
            q = ag * q + u_scr[n][step_rows(g), :]
            p = ag * p
        p, q = _sublane_scan(p, q, sub)
        carry = hc_scr[:, cs]
        seg_end = q + p * carry
        h = jnp.where(sub == 0, carry, pltpu.roll(seg_end, 1, axis=0))
        hc_scr[:, cs] = jnp.broadcast_to(seg_end[SUBLANES - 1:SUBLANES], seg_end.shape)
        for g in range(SEG):
            h = a_scr[n][step_rows(g), :] * h + u_scr[n][step_rows(g), :]
            h_scr[n][seg_step(g), :] = h

    for n in range(N_REC_BLOCKS):
        cs = slice(n * REC_BLOCK, (n + 1) * REC_BLOCK)
        h = jnp.concatenate([h_scr[n][seg_rows(s), :] for s in range(SUBLANES)], axis=0)
        gr = gr_ref[:, cs].astype(F32)
        o_ref[:, cs] = (h * (gr * _sigmoid(gr))).astype(BF16)


def _rglru(proj, conv_w, conv_b, w_gate, b_a, b_x, lru_lambda, bsz, seq):
    n_t = seq // TS
    xr_group, gr_group = 4, 5
    full = lambda shape: pl.BlockSpec(shape, lambda b, t: (0,) * len(shape))
    slab = pltpu.VMEM((SUBLANES * SEG_PITCH, REC_BLOCK), F32)
    plain = pltpu.VMEM((TS, REC_BLOCK), F32)
    return pl.pallas_call(
        _rglru_kernel,
        grid=(bsz, n_t),
        in_specs=[
            pl.BlockSpec((TS, REC_WIDTH), lambda b, t: (b * n_t + t, xr_group)),
            pl.BlockSpec((TS, REC_WIDTH), lambda b, t: (b * n_t + t, gr_group)),
            full((CONV_WIDTH, REC_WIDTH)),
            full((1, REC_WIDTH)),
            full((N_REC_BLOCKS, REC_BLOCK, 2 * REC_BLOCK)),
            full((1, REC_WIDTH)), full((1, REC_WIDTH)), full((1, REC_WIDTH)),
        ],
        out_specs=pl.BlockSpec((TS, REC_WIDTH), lambda b, t: (b * n_t + t, 0)),
        out_shape=jax.ShapeDtypeStruct((bsz * seq, REC_WIDTH), BF16),
        scratch_shapes=[
            pltpu.VMEM((CONV_WIDTH - 1, SUBLANES, REC_WIDTH), F32),
            pltpu.VMEM((SUBLANES, REC_WIDTH), F32),
        ] + [slab] * (2 * N_REC_BLOCKS) + [plain] * (2 * N_REC_BLOCKS),
        compiler_params=pltpu.CompilerParams(
            dimension_semantics=("arbitrary", "arbitrary"),
            vmem_limit_bytes=VMEM_LIMIT),
        name="conv_rglru",
    )(proj, proj, conv_w, conv_b, w_gate, b_a, b_x, lru_lambda)


def _outproj_kernel(x_ref, o_ref, r_ref, w_ref, gain_ref, y_ref, w_scr):
    @pl.when(pl.program_id(0) == 0)
    def _():
        w_scr[...] = w_ref[...].astype(BF16)

    y = x_ref[...]
    y = y + jnp.dot(o_ref[...], w_scr[0:ATTN_WIDTH, :], preferred_element_type=F32)
    y = y + jnp.dot(r_ref[...], w_scr[ATTN_WIDTH:, :], preferred_element_type=F32)
    ms = jnp.mean(y * y, axis=-1, keepdims=True)
    y_ref[...] = y * lax.rsqrt(ms + NORM_EPS) * gain_ref[...]


def _outproj(x2, o, r, w_out, final_gain):
    m = x2.shape[0]
    return pl.pallas_call(
        _outproj_kernel,
        grid=(m // TM_OUT,),
        in_specs=[
            pl.BlockSpec((TM_OUT, D_MODEL), lambda i: (i, 0)),
            pl.BlockSpec((TM_OUT, ATTN_WIDTH), lambda i: (i, 0)),
            pl.BlockSpec((TM_OUT, REC_WIDTH), lambda i: (i, 0)),
            pl.BlockSpec((ATTN_WIDTH + REC_WIDTH, D_MODEL), lambda i: (0, 0)),
            pl.BlockSpec((1, D_MODEL), lambda i: (0, 0)),
        ],
        out_specs=pl.BlockSpec((TM_OUT, D_MODEL), lambda i: (i, 0)),
        out_shape=jax.ShapeDtypeStruct((m, D_MODEL), F32),
        scratch_shapes=[pltpu.VMEM((ATTN_WIDTH + REC_WIDTH, D_MODEL), BF16)],
        compiler_params=pltpu.CompilerParams(
            dimension_semantics=("arbitrary",),
            vmem_limit_bytes=VMEM_LIMIT),
        name="outproj_norm",
    )(x2, o, r, w_out, final_gain)


def kernel(x, positions, norm_gain, w_in, lambda_q1, lambda_k1, lambda_q2, lambda_k2,
           subln_gain, conv_w, conv_b, w_a, b_a, w_x, b_x, lru_lambda, w_out, final_gain):
    bsz, seq, _ = x.shape
    assert x.shape[2] == D_MODEL and norm_gain.shape[0] == 1
    assert seq % max(TQ, TS) == 0 and (bsz * seq) % max(TM_IN, TM_OUT) == 0
    x2 = x.reshape(bsz * seq, D_MODEL)
    pos2 = positions.reshape(bsz * seq // LANES, LANES)
    inv_freq = ROPE_THETA ** (-jnp.arange(0, QK_DIM, 2, dtype=F32) / QK_DIM)
    invf = jnp.tile(inv_freq, LANES // ROT_HALF).reshape(1, LANES)

    proj = _inproj(x2, pos2, invf, norm_gain[0].reshape(1, D_MODEL), w_in[0])
    o = _attention(proj, lambda_q1[0].reshape(1, QK_DIM), lambda_k1[0].reshape(1, QK_DIM),
                   lambda_q2[0].reshape(1, QK_DIM), lambda_k2[0].reshape(1, QK_DIM),
                   subln_gain[0].reshape(1, HEAD_DIM), bsz, seq)
    w_gate = jnp.concatenate([w_a[0], w_x[0]], axis=-1).astype(BF16)
    r = _rglru(proj, conv_w[0], conv_b[0].reshape(1, REC_WIDTH), w_gate,
               b_a[0].reshape(1, REC_WIDTH), b_x[0].reshape(1, REC_WIDTH),
               lru_lambda[0].reshape(1, REC_WIDTH), bsz, seq)
    y = _outproj(x2, o, r, w_out[0], final_gain.reshape(1, D_MODEL))
    return y.reshape(bsz, seq, D_MODEL)
```

```python
import math

import jax
import jax.numpy as jnp
from jax import lax
from jax.experimental import pallas as pl
from jax.experimental.pallas import tpu as pltpu

F32 = jnp.float32
BF16 = jnp.bfloat16

D_MODEL = 2048
ATTN_WIDTH = D_MODEL // 2
REC_WIDTH = D_MODEL // 2
N_ATTN_HEADS = 8
HEAD_DIM = ATTN_WIDTH // N_ATTN_HEADS
QK_DIM = HEAD_DIM // 2
ROT_HALF = QK_DIM // 2
N_REC_BLOCKS = 8
REC_BLOCK = REC_WIDTH // N_REC_BLOCKS
CONV_WIDTH = 4
LRU_C = 8.0
ROPE_THETA = 10000.0
NORM_EPS = 1e-6
IN_WIDTH = 4 * ATTN_WIDTH + 2 * REC_WIDTH
LAM_INIT = 0.8 - 0.6 * math.exp(-0.3 * 0)
QK_SCALE = QK_DIM ** -0.5
LOG2E = math.log2(math.e)

SUBLANES = 8
LANES = 128

TM_IN = 512
TN_IN = 1024
TQ = 256
SOFTMAX_ROWS = 64
TS = 256
TM_OUT = 512
VMEM_LIMIT = 48 * 1024 * 1024
VMEM_LIMIT_INPROJ = 58 * 1024 * 1024


def _sigmoid(x):
    return 0.5 * jnp.tanh(0.5 * x) + 0.5


N_COL_TILES = IN_WIDTH // TN_IN
N_ROPE_TILES = 2 * ATTN_WIDTH // TN_IN
N_PLAIN_TILES = N_COL_TILES - N_ROPE_TILES


def _col_tile(j):
    return (j + N_ROPE_TILES) % N_COL_TILES


def _rope_partner(blk, lane):
    fwd = pltpu.roll(blk, LANES - ROT_HALF, axis=1)
    bwd = pltpu.roll(blk, ROT_HALF, axis=1)
    return jnp.where((lane % QK_DIM) < ROT_HALF, fwd, bwd)


def _inproj_kernel(x_ref, pos_ref, invf_ref, gain_ref, w_ref, o_ref,
                   w_scr, h_scr, rstd_scr, cos_scr, sin_scr):
    i = pl.program_id(0)
    j = pl.program_id(1)
    tm = x_ref.shape[0]

    @pl.when(i == 0)
    def _():
        w_scr[j] = w_ref[...].astype(BF16)

    def plain_store(h, rstd):
        rstd2 = jnp.concatenate([rstd, rstd], axis=1)
        for pair in range(TN_IN // (2 * LANES)):
            cs = slice(pair * 2 * LANES, (pair + 1) * 2 * LANES)
            acc = jnp.dot(h, w_scr[j, :, cs], preferred_element_type=F32)
            o_ref[:, cs] = (acc * rstd2).astype(BF16)

    @pl.when(j == 0)
    def _():
        x = x_ref[...]
        h = (x * gain_ref[...]).astype(BF16)
        h_scr[...] = h
        rstd = lax.rsqrt(jnp.mean(x * x, axis=-1, keepdims=True) + NORM_EPS)
        rstd = jnp.broadcast_to(rstd, (tm, LANES))
        rstd_scr[...] = rstd
        plain_store(h, rstd)
        ang = pos_ref[...].astype(F32) * invf_ref[...]
        lane = lax.broadcasted_iota(jnp.int32, (tm, LANES), 1)
        sign = jnp.where((lane % QK_DIM) < ROT_HALF, -1.0, 1.0).astype(F32)
        cos_scr[...] = jnp.cos(ang)
        sin_scr[...] = jnp.sin(ang) * sign

    @pl.when(jnp.logical_and(j > 0, j < N_PLAIN_TILES))
    def _():
        plain_store(h_scr[...], rstd_scr[...])

    @pl.when(j >= N_PLAIN_TILES)
    def _():
        scale = jnp.where(j == N_PLAIN_TILES, QK_SCALE * LOG2E, 1.0).astype(F32)
        rstd = rstd_scr[...]
        cosf = cos_scr[...] * scale
        sinf = sin_scr[...] * scale
        lane = lax.broadcasted_iota(jnp.int32, (tm, LANES), 1)
        for pair in range(TN_IN // (2 * LANES)):
            c0 = pair * 2 * LANES
            acc = jnp.dot(h_scr[...], w_scr[j, :, c0:c0 + 2 * LANES],
                          preferred_element_type=F32)
            for half in range(2):
                blk = acc[:, half * LANES:(half + 1) * LANES] * rstd
                rot = blk * cosf + _rope_partner(blk, lane) * sinf
                o_ref[:, c0 + half * LANES:c0 + (half + 1) * LANES] = rot.astype(BF16)


def _inproj(x2, pos2, invf, gain, w_in):
    m = x2.shape[0]
    grid = (m // TM_IN, N_COL_TILES)
    last_tile = _col_tile(N_COL_TILES - 1)
    return pl.pallas_call(
        _inproj_kernel,
        grid=grid,
        in_specs=[
            pl.BlockSpec((TM_IN, D_MODEL), lambda i, j: (i, 0)),
            pl.BlockSpec((TM_IN, 1), lambda i, j: (i, 0)),
            pl.BlockSpec((1, LANES), lambda i, j: (0, 0)),
            pl.BlockSpec((1, D_MODEL), lambda i, j: (0, 0)),
            pl.BlockSpec((D_MODEL, TN_IN),
                         lambda i, j: (0, jnp.where(i == 0, _col_tile(j), last_tile))),
        ],
        out_specs=pl.BlockSpec((TM_IN, TN_IN), lambda i, j: (i, _col_tile(j))),
        out_shape=jax.ShapeDtypeStruct((m, IN_WIDTH), BF16),
        scratch_shapes=[
            pltpu.VMEM((N_COL_TILES, D_MODEL, TN_IN), BF16),
            pltpu.VMEM((TM_IN, D_MODEL), BF16),
            pltpu.VMEM((TM_IN, LANES), F32),
            pltpu.VMEM((TM_IN, LANES), F32),
            pltpu.VMEM((TM_IN, LANES), F32),
        ],
        compiler_params=pltpu.CompilerParams(
            dimension_semantics=("arbitrary", "arbitrary"),
            vmem_limit_bytes=VMEM_LIMIT_INPROJ),
        name="inproj_rope",
    )(x2, pos2, invf, gain, w_in)


def _dot_nt(a, b):
    return lax.dot_general(a, b, (((1,), (1,)), ((), ())), preferred_element_type=F32)


def _attn_kernel(q_ref, k_ref, v_ref, g_ref, lq1_ref, lk1_ref, lq2_ref, lk2_ref,
                 sg_ref, o_ref, s_scr, p_scr):
    seq = q_ref.shape[0]
    lam = (jnp.exp(jnp.sum(lq1_ref[...] * lk1_ref[...], axis=-1, keepdims=True))
           - jnp.exp(jnp.sum(lq2_ref[...] * lk2_ref[...], axis=-1, keepdims=True))
           + LAM_INIT)
    lane = lax.broadcasted_iota(jnp.int32, (TQ, LANES), 1)
    row = lax.broadcasted_iota(jnp.int32, (2 * TQ, TQ), 0)
    col = lax.broadcasted_iota(jnp.int32, (2 * TQ, TQ), 1)
    causal = col <= jnp.where(row >= TQ, row - TQ, row)

    for i in range(seq // TQ):
        buf = i % 2
        w0, w = i * TQ, (i + 1) * TQ
        q = q_ref[w0:w, :]
        zero = jnp.zeros_like(q)
        qcat = jnp.concatenate([jnp.where(lane < QK_DIM, q, zero),
                                jnp.where(lane < QK_DIM, zero, q)], axis=0)
        if i > 0:
            s_scr[buf, :, 0:w0] = _dot_nt(qcat, k_ref[0:w0, :])
        s_scr[buf, :, w0:w] = jnp.where(causal, _dot_nt(qcat, k_ref[w0:w, :]), -jnp.inf)

        for rb in range(2 * TQ // SOFTMAX_ROWS):
            rows = slice(rb * SOFTMAX_ROWS, (rb + 1) * SOFTMAX_ROWS)
            m = jnp.max(s_scr[buf, rows, 0:w], axis=1, keepdims=True)
            p_scr[buf, rows, 0:w] = jnp.exp2(s_scr[buf, rows, 0:w] - m).astype(BF16)

        v_ext = jnp.concatenate([v_ref[0:w, :], jnp.ones((w, HEAD_DIM), BF16)], axis=1)
        acc1 = jnp.dot(p_scr[buf, 0:TQ, 0:w], v_ext, preferred_element_type=F32)
        acc2 = jnp.dot(p_scr[buf, TQ:, 0:w], v_ext, preferred_element_type=F32)
        o = (acc1[:, :HEAD_DIM] * (1.0 / acc1[:, HEAD_DIM:])
             - acc2[:, :HEAD_DIM] * (lam / acc2[:, HEAD_DIM:]))
        ms = jnp.mean(o * o, axis=-1, keepdims=True)
        o = o * lax.rsqrt(ms + NORM_EPS) * sg_ref[...] * (1.0 - LAM_INIT)
        g = g_ref[w0:w, :].astype(F32)
        o_ref[w0:w, :] = (o * (g * _sigmoid(g))).astype(BF16)


def _attention(proj, lq1, lk1, lq2, lk2, subln, bsz, seq):
    head_block = (seq, HEAD_DIM)
    q_group, k_group, v_group, g_group = (g * N_ATTN_HEADS for g in range(4))
    small = lambda w: pl.BlockSpec((1, w), lambda b, h: (0, 0))
    return pl.pallas_call(
        _attn_kernel,
        grid=(bsz, N_ATTN_HEADS),
        in_specs=[
            pl.BlockSpec(head_block, lambda b, h: (b, q_group + h)),
            pl.BlockSpec(head_block, lambda b, h: (b, k_group + h)),
            pl.BlockSpec(head_block, lambda b, h: (b, v_group + h)),
            pl.BlockSpec(head_block, lambda b, h: (b, g_group + h)),
            small(QK_DIM), small(QK_DIM), small(QK_DIM), small(QK_DIM),
            small(HEAD_DIM),
        ],
        out_specs=pl.BlockSpec(head_block, lambda b, h: (b, h)),
        out_shape=jax.ShapeDtypeStruct((bsz * seq, ATTN_WIDTH), BF16),
        scratch_shapes=[pltpu.VMEM((2, 2 * TQ, seq), F32),
                        pltpu.VMEM((2, 2 * TQ, seq), BF16)],
        compiler_params=pltpu.CompilerParams(
            dimension_semantics=("arbitrary", "arbitrary"),
            vmem_limit_bytes=VMEM_LIMIT),
        name="diff_attention",
    )(proj, proj, proj, proj, lq1, lk1, lq2, lk2, subln)


SEG = TS // SUBLANES
SEG_PITCH = SEG + SUBLANES


def _sublane_scan(p, q, sub):
    d = 1
    while d < SUBLANES:
        keep = sub >= d
        p_prev = pltpu.roll(p, d, axis=0)
        q_prev = pltpu.roll(q, d, axis=0)
        q = jnp.where(keep, p * q_prev, 0.0) + q
        p = jnp.where(keep, p * p_prev, p)
        d *= 2
    return p, q


def _rglru_kernel(xr_ref, gr_ref, cw_ref, cb_ref, wg_ref, ba_ref, bx_ref, lam_ref,
                  o_ref, tail_scr, hc_scr, *scratch):
    x_scr, h_scr, a_scr, u_scr = (scratch[k * N_REC_BLOCKS:(k + 1) * N_REC_BLOCKS]
                                  for k in range(4))
    t = pl.program_id(1)

    @pl.when(t == 0)
    def _():
        tail_scr[...] = jnp.zeros_like(tail_scr)
        hc_scr[...] = jnp.zeros_like(hc_scr)

    sub = lax.broadcasted_iota(jnp.int32, (SUBLANES, REC_BLOCK), 0)
    seg_rows = lambda s: slice(s * SEG_PITCH, s * SEG_PITCH + SEG)
    seg_step = lambda g: pl.ds(g, SUBLANES, stride=SEG_PITCH)
    step_rows = lambda g: slice(g * SUBLANES, (g + 1) * SUBLANES)

    for n in range(N_REC_BLOCKS):
        xf = xr_ref[:, n * REC_BLOCK:(n + 1) * REC_BLOCK].astype(F32)
        for s in range(SUBLANES):
            x_scr[n][seg_rows(s), :] = xf[s * SEG:(s + 1) * SEG]

    for n in range(N_REC_BLOCKS):
        cs = slice(n * REC_BLOCK, (n + 1) * REC_BLOCK)
        xs = [x_scr[n][seg_step(g), :] for g in range(SEG)]
        before = {}
        for j in range(1, CONV_WIDTH):
            cur = pltpu.roll(xs[SEG - j], 1, axis=0)
            before[-j] = jnp.where(sub == 0, tail_scr[j - 1, :, cs], cur)
            tail_scr[j - 1, :, cs] = cur
        x_at = lambda g: xs[g] if g >= 0 else before[g]
        taps = [cw_ref[CONV_WIDTH - 1 - j:CONV_WIDTH - j, cs] for j in range(CONV_WIDTH)]
        blocks = []
        for g in range(SEG):
            yb = cb_ref[:, cs]
            for j in range(CONV_WIDTH):
                yb = yb + taps[j] * x_at(g - j)
            blocks.append(yb)
        y = jnp.concatenate(blocks, axis=0)

        gates = jnp.dot(y.astype(BF16), wg_ref[n], preferred_element_type=F32)
        r = _sigmoid(gates[:, :REC_BLOCK] + ba_ref[:, cs])
        ig = _sigmoid(gates[:, REC_BLOCK:] + bx_ref[:, cs])
        log_a = r * (-LRU_C * jax.nn.softplus(-lam_ref[:, cs]))
        a = jnp.exp(log_a)
        gap = -jnp.tanh(log_a) * (a * a + 1.0)
        mult = jnp.where(gap > 0.0, gap * lax.rsqrt(gap), 0.0)
        a_scr[n][...] = a
        u_scr[n][...] = mult * (ig * y)

    for n in range(N_REC_BLOCKS):
        cs = slice(n * REC_BLOCK, (n + 1) * REC_BLOCK)
        p = jnp.ones((SUBLANES, REC_BLOCK), F32)
        q = jnp.zeros((SUBLANES, REC_BLOCK), F32)
        for g in range(SEG):
            ag = a_scr[n][step_rows(g), :]
            q = ag * q + u_scr[n][step_rows(g), :]
            p = ag * p
        p, q = _sublane_scan(p, q, sub)
        carry = hc_scr[:, cs]
        seg_end = q + p * carry
        h = jnp.where(sub == 0, carry, pltpu.roll(seg_end, 1, axis=0))
        hc_scr[:, cs] = jnp.broadcast_to(seg_end[SUBLANES - 1:SUBLANES], seg_end.shape)
        for g in range(SEG):
            h = a_scr[n][step_rows(g), :] * h + u_scr[n][step_rows(g), :]
            h_scr[n][seg_step(g), :] = h

    for n in range(N_REC_BLOCKS):
        cs = slice(n * REC_BLOCK, (n + 1) * REC_BLOCK)
        h = jnp.concatenate([h_scr[n][seg_rows(s), :] for s in range(SUBLANES)], axis=0)
        gr = gr_ref[:, cs].astype(F32)
        o_ref[:, cs] = (h * (gr * _sigmoid(gr))).astype(BF16)


def _rglru(proj, conv_w, conv_b, w_gate, b_a, b_x, lru_lambda, bsz, seq):
    n_t = seq // TS
    xr_group, gr_group = 4, 5
    full = lambda shape: pl.BlockSpec(shape, lambda b, t: (0,) * len(shape))
    slab = pltpu.VMEM((SUBLANES * SEG_PITCH, REC_BLOCK), F32)
    plain = pltpu.VMEM((TS, REC_BLOCK), F32)
    return pl.pallas_call(
        _rglru_kernel,
        grid=(bsz, n_t),
        in_specs=[
            pl.BlockSpec((TS, REC_WIDTH), lambda b, t: (b * n_t + t, xr_group)),
            pl.BlockSpec((TS, REC_WIDTH), lambda b, t: (b * n_t + t, gr_group)),
            full((CONV_WIDTH, REC_WIDTH)),
            full((1, REC_WIDTH)),
            full((N_REC_BLOCKS, REC_BLOCK, 2 * REC_BLOCK)),
            full((1, REC_WIDTH)), full((1, REC_WIDTH)), full((1, REC_WIDTH)),
        ],
        out_specs=pl.BlockSpec((TS, REC_WIDTH), lambda b, t: (b * n_t + t, 0)),
        out_shape=jax.ShapeDtypeStruct((bsz * seq, REC_WIDTH), BF16),
        scratch_shapes=[
            pltpu.VMEM((CONV_WIDTH - 1, SUBLANES, REC_WIDTH), F32),
            pltpu.VMEM((SUBLANES, REC_WIDTH), F32),
        ] + [slab] * (2 * N_REC_BLOCKS) + [plain] * (2 * N_REC_BLOCKS),
        compiler_params=pltpu.CompilerParams(
            dimension_semantics=("arbitrary", "arbitrary"),
            vmem_limit_bytes=VMEM_LIMIT),
        name="conv_rglru",
    )(proj, proj, conv_w, conv_b, w_gate, b_a, b_x, lru_lambda)


def _outproj_kernel(x_ref, o_ref, r_ref, w_ref, gain_ref, y_ref, w_scr):
    @pl.when(pl.program_id(0) == 0)
    def _():
        w_scr[...] = w_ref[...].astype(BF16)

    y = x_ref[...]
    y = y + jnp.dot(o_ref[...], w_scr[0:ATTN_WIDTH, :], preferred_element_type=F32)
    y = y + jnp.dot(r_ref[...], w_scr[ATTN_WIDTH:, :], preferred_element_type=F32)
    ms = jnp.mean(y * y, axis=-1, keepdims=True)
    y_ref[...] = y * lax.rsqrt(ms + NORM_EPS) * gain_ref[...]


def _outproj(x2, o, r, w_out, final_gain):
    m = x2.shape[0]
    return pl.pallas_call(
        _outproj_kernel,
        grid=(m // TM_OUT,),
        in_specs=[
            pl.BlockSpec((TM_OUT, D_MODEL), lambda i: (i, 0)),
            pl.BlockSpec((TM_OUT, ATTN_WIDTH), lambda i: (i, 0)),
            pl.BlockSpec((TM_OUT, REC_WIDTH), lambda i: (i, 0)),
            pl.BlockSpec((ATTN_WIDTH + REC_WIDTH, D_MODEL), lambda i: (0, 0)),
            pl.BlockSpec((1, D_MODEL), lambda i: (0, 0)),
        ],
        out_specs=pl.BlockSpec((TM_OUT, D_MODEL), lambda i: (i, 0)),
        out_shape=jax.ShapeDtypeStruct((m, D_MODEL), F32),
        scratch_shapes=[pltpu.VMEM((ATTN_WIDTH + REC_WIDTH, D_MODEL), BF16)],
        compiler_params=pltpu.CompilerParams(
            dimension_semantics=("arbitrary",),
            vmem_limit_bytes=VMEM_LIMIT),
        name="outproj_norm",
    )(x2, o, r, w_out, final_gain)


def kernel(x, positions, norm_gain, w_in, lambda_q1, lambda_k1, lambda_q2, lambda_k2,
           subln_gain, conv_w, conv_b, w_a, b_a, w_x, b_x, lru_lambda, w_out, final_gain):
    bsz, seq, _ = x.shape
    assert x.shape[2] == D_MODEL and norm_gain.shape[0] == 1
    assert seq % max(TQ, TS) == 0 and (bsz * seq) % max(TM_IN, TM_OUT) == 0
    x2 = x.reshape(bsz * seq, D_MODEL)
    pos2 = positions.reshape(bsz * seq, 1)
    inv_freq = ROPE_THETA ** (-jnp.arange(0, QK_DIM, 2, dtype=F32) / QK_DIM)
    invf = jnp.tile(inv_freq, LANES // ROT_HALF).reshape(1, LANES)

    proj = _inproj(x2, pos2, invf, norm_gain[0].reshape(1, D_MODEL), w_in[0])
    o = _attention(proj, lambda_q1[0].reshape(1, QK_DIM), lambda_k1[0].reshape(1, QK_DIM),
                   lambda_q2[0].reshape(1, QK_DIM), lambda_k2[0].reshape(1, QK_DIM),
                   subln_gain[0].reshape(1, HEAD_DIM), bsz, seq)
    w_gate = jnp.concatenate([w_a[0], w_x[0]], axis=-1).astype(BF16)
    r = _rglru(proj, conv_w[0], conv_b[0].reshape(1, REC_WIDTH), w_gate,
               b_a[0].reshape(1, REC_WIDTH), b_x[0].reshape(1, REC_WIDTH),
               lru_lambda[0].reshape(1, REC_WIDTH), bsz, seq)
    y = _outproj(x2, o, r, w_out[0], final_gain.reshape(1, D_MODEL))
    return y.reshape(bsz, seq, D_MODEL)
```

```python
import math

import jax
import jax.numpy as jnp
from jax import lax
from jax.experimental import pallas as pl
from jax.experimental.pallas import tpu as pltpu

F32 = jnp.float32
BF16 = jnp.bfloat16

D_MODEL = 2048
ATTN_WIDTH = D_MODEL // 2
REC_WIDTH = D_MODEL // 2
N_ATTN_HEADS = 8
HEAD_DIM = ATTN_WIDTH // N_ATTN_HEADS
QK_DIM = HEAD_DIM // 2
ROT_HALF = QK_DIM // 2
N_REC_BLOCKS = 8
REC_BLOCK = REC_WIDTH // N_REC_BLOCKS
CONV_WIDTH = 4
LRU_C = 8.0
ROPE_THETA = 10000.0
NORM_EPS = 1e-6
IN_WIDTH = 4 * ATTN_WIDTH + 2 * REC_WIDTH
LAM_INIT = 0.8 - 0.6 * math.exp(-0.3 * 0)
QK_SCALE = QK_DIM ** -0.5
LOG2E = math.log2(math.e)

SUBLANES = 8
LANES = 128

TM_IN = 512
TN_IN = 1024
TQ = 256
SOFTMAX_ROWS = 64
HEADS_PER_STEP = 2
TS = 256
TM_OUT = 512
VMEM_LIMIT = 48 * 1024 * 1024
VMEM_LIMIT_INPROJ = 58 * 1024 * 1024


def _sigmoid(x):
    return 0.5 * jnp.tanh(0.5 * x) + 0.5


N_COL_TILES = IN_WIDTH // TN_IN
N_ROPE_TILES = 2 * ATTN_WIDTH // TN_IN
N_PLAIN_TILES = N_COL_TILES - N_ROPE_TILES


def _col_tile(j):
    return (j + N_ROPE_TILES) % N_COL_TILES


def _rope_partner(blk, lane):
    fwd = pltpu.roll(blk, LANES - ROT_HALF, axis=1)
    bwd = pltpu.roll(blk, ROT_HALF, axis=1)
    return jnp.where((lane % QK_DIM) < ROT_HALF, fwd, bwd)


def _inproj_kernel(x_ref, pos_ref, invf_ref, gain_ref, w_ref, o_ref,
                   w_scr, h_scr, rstd_scr, cos_scr, sin_scr):
    i = pl.program_id(0)
    j = pl.program_id(1)
    tm = x_ref.shape[0]

    @pl.when(i == 0)
    def _():
        w_scr[j] = w_ref[...].astype(BF16)

    def plain_store(h, rstd):
        rstd2 = jnp.concatenate([rstd, rstd], axis=1)
        for pair in range(TN_IN // (2 * LANES)):
            cs = slice(pair * 2 * LANES, (pair + 1) * 2 * LANES)
            acc = jnp.dot(h, w_scr[j, :, cs], preferred_element_type=F32)
            o_ref[:, cs] = (acc * rstd2).astype(BF16)

    @pl.when(j == 0)
    def _():
        x = x_ref[...]
        h = (x * gain_ref[...]).astype(BF16)
        h_scr[...] = h
        rstd = lax.rsqrt(jnp.mean(x * x, axis=-1, keepdims=True) + NORM_EPS)
        rstd = jnp.broadcast_to(rstd, (tm, LANES))
        rstd_scr[...] = rstd
        plain_store(h, rstd)
        ang = pos_ref[...].astype(F32) * invf_ref[...]
        lane = lax.broadcasted_iota(jnp.int32, (tm, LANES), 1)
        sign = jnp.where((lane % QK_DIM) < ROT_HALF, -1.0, 1.0).astype(F32)
        cos_scr[...] = jnp.cos(ang)
        sin_scr[...] = jnp.sin(ang) * sign

    @pl.when(jnp.logical_and(j > 0, j < N_PLAIN_TILES))
    def _():
        plain_store(h_scr[...], rstd_scr[...])

    @pl.when(j >= N_PLAIN_TILES)
    def _():
        scale = jnp.where(j == N_PLAIN_TILES, QK_SCALE * LOG2E, 1.0).astype(F32)
        rstd = rstd_scr[...]
        cosf = cos_scr[...] * scale
        sinf = sin_scr[...] * scale
        lane = lax.broadcasted_iota(jnp.int32, (tm, LANES), 1)
        for pair in range(TN_IN // (2 * LANES)):
            c0 = pair * 2 * LANES
            acc = jnp.dot(h_scr[...], w_scr[j, :, c0:c0 + 2 * LANES],
                          preferred_element_type=F32)
            for half in range(2):
                blk = acc[:, half * LANES:(half + 1) * LANES] * rstd
                rot = blk * cosf + _rope_partner(blk, lane) * sinf
                o_ref[:, c0 + half * LANES:c0 + (half + 1) * LANES] = rot.astype(BF16)


def _inproj(x2, pos2, invf, gain, w_in):
    m = x2.shape[0]
    grid = (m // TM_IN, N_COL_TILES)
    last_tile = _col_tile(N_COL_TILES - 1)
    return pl.pallas_call(
        _inproj_kernel,
        grid=grid,
        in_specs=[
            pl.BlockSpec((TM_IN, D_MODEL), lambda i, j: (i, 0)),
            pl.BlockSpec((TM_IN, 1), lambda i, j: (i, 0)),
            pl.BlockSpec((1, LANES), lambda i, j: (0, 0)),
            pl.BlockSpec((1, D_MODEL), lambda i, j: (0, 0)),
            pl.BlockSpec((D_MODEL, TN_IN),
                         lambda i, j: (0, jnp.where(i == 0, _col_tile(j), last_tile))),
        ],
        out_specs=pl.BlockSpec((TM_IN, TN_IN), lambda i, j: (i, _col_tile(j))),
        out_shape=jax.ShapeDtypeStruct((m, IN_WIDTH), BF16),
        scratch_shapes=[
            pltpu.VMEM((N_COL_TILES, D_MODEL, TN_IN), BF16),
            pltpu.VMEM((TM_IN, D_MODEL), BF16),
            pltpu.VMEM((TM_IN, LANES), F32),
            pltpu.VMEM((TM_IN, LANES), F32),
            pltpu.VMEM((TM_IN, LANES), F32),
        ],
        compiler_params=pltpu.CompilerParams(
            dimension_semantics=("arbitrary", "arbitrary"),
            vmem_limit_bytes=VMEM_LIMIT_INPROJ),
        name="inproj_rope",
    )(x2, pos2, invf, gain, w_in)


def _dot_nt(a, b):
    return lax.dot_general(a, b, (((1,), (1,)), ((), ())), preferred_element_type=F32)


def _attn_kernel(q_ref, k_ref, v_ref, g_ref, lq1_ref, lk1_ref, lq2_ref, lk2_ref,
                 sg_ref, o_ref, s_scr, p_scr):
    seq = q_ref.shape[0]
    lam = (jnp.exp(jnp.sum(lq1_ref[...] * lk1_ref[...], axis=-1, keepdims=True))
           - jnp.exp(jnp.sum(lq2_ref[...] * lk2_ref[...], axis=-1, keepdims=True))
           + LAM_INIT)
    lane = lax.broadcasted_iota(jnp.int32, (TQ, LANES), 1)
    row = lax.broadcasted_iota(jnp.int32, (2 * TQ, TQ), 0)
    col = lax.broadcasted_iota(jnp.int32, (2 * TQ, TQ), 1)
    causal = col <= jnp.where(row >= TQ, row - TQ, row)

    n_q = seq // TQ
    for tile in range(HEADS_PER_STEP * n_q):
        head, i = divmod(tile, n_q)
        hl = slice(head * HEAD_DIM, (head + 1) * HEAD_DIM)
        buf = tile % 2
        w0, w = i * TQ, (i + 1) * TQ
        q = q_ref[w0:w, hl]
        zero = jnp.zeros_like(q)
        qcat = jnp.concatenate([jnp.where(lane < QK_DIM, q, zero),
                                jnp.where(lane < QK_DIM, zero, q)], axis=0)
        if i > 0:
            s_scr[buf, :, 0:w0] = _dot_nt(qcat, k_ref[0:w0, hl])
        s_scr[buf, :, w0:w] = jnp.where(causal, _dot_nt(qcat, k_ref[w0:w, hl]), -jnp.inf)

        for rb in range(2 * TQ // SOFTMAX_ROWS):
            rows = slice(rb * SOFTMAX_ROWS, (rb + 1) * SOFTMAX_ROWS)
            m = jnp.max(s_scr[buf, rows, 0:w], axis=1, keepdims=True)
            p_scr[buf, rows, 0:w] = jnp.exp2(s_scr[buf, rows, 0:w] - m).astype(BF16)

        v_ext = jnp.concatenate([v_ref[0:w, hl], jnp.ones((w, HEAD_DIM), BF16)], axis=1)
        acc1 = jnp.dot(p_scr[buf, 0:TQ, 0:w], v_ext, preferred_element_type=F32)
        acc2 = jnp.dot(p_scr[buf, TQ:, 0:w], v_ext, preferred_element_type=F32)
        o = (acc1[:, :HEAD_DIM] * (1.0 / acc1[:, HEAD_DIM:])
             - acc2[:, :HEAD_DIM] * (lam / acc2[:, HEAD_DIM:]))
        ms = jnp.mean(o * o, axis=-1, keepdims=True)
        o = o * lax.rsqrt(ms + NORM_EPS) * sg_ref[...] * (1.0 - LAM_INIT)
        g = g_ref[w0:w, hl].astype(F32)
        o_ref[w0:w, hl] = (o * (g * _sigmoid(g))).astype(BF16)


def _attention(proj, lq1, lk1, lq2, lk2, subln, bsz, seq):
    head_block = (seq, HEADS_PER_STEP * HEAD_DIM)
    n_steps = N_ATTN_HEADS // HEADS_PER_STEP
    q_group, k_group, v_group, g_group = (g * n_steps for g in range(4))
    small = lambda w: pl.BlockSpec((1, w), lambda b, h: (0, 0))
    return pl.pallas_call(
        _attn_kernel,
        grid=(bsz, n_steps),
        in_specs=[
            pl.BlockSpec(head_block, lambda b, h: (b, q_group + h)),
            pl.BlockSpec(head_block, lambda b, h: (b, k_group + h)),
            pl.BlockSpec(head_block, lambda b, h: (b, v_group + h)),
            pl.BlockSpec(head_block, lambda b, h: (b, g_group + h)),
            small(QK_DIM), small(QK_DIM), small(QK_DIM), small(QK_DIM),
            small(HEAD_DIM),
        ],
        out_specs=pl.BlockSpec(head_block, lambda b, h: (b, h)),
        out_shape=jax.ShapeDtypeStruct((bsz * seq, ATTN_WIDTH), BF16),
        scratch_shapes=[pltpu.VMEM((2, 2 * TQ, seq), F32),
                        pltpu.VMEM((2, 2 * TQ, seq), BF16)],
        compiler_params=pltpu.CompilerParams(
            dimension_semantics=("arbitrary", "arbitrary"),
            vmem_limit_bytes=VMEM_LIMIT),
        name="diff_attention",
    )(proj, proj, proj, proj, lq1, lk1, lq2, lk2, subln)


SEG = TS // SUBLANES
SEG_PITCH = SEG + SUBLANES


def _sublane_scan(p, q, sub):
    d = 1
    while d < SUBLANES:
        keep = sub >= d
        p_prev = pltpu.roll(p, d, axis=0)
        q_prev = pltpu.roll(q, d, axis=0)
        q = jnp.where(keep, p * q_prev, 0.0) + q
        p = jnp.where(keep, p * p_prev, p)
        d *= 2
    return p, q


def _rglru_kernel(xr_ref, gr_ref, cw_ref, cb_ref, wg_ref, ba_ref, bx_ref, lam_ref,
                  o_ref, tail_scr, hc_scr, *scratch):
    x_scr, h_scr, a_scr, u_scr = (scratch[k * N_REC_BLOCKS:(k + 1) * N_REC_BLOCKS]
                                  for k in range(4))
    t = pl.program_id(1)

    @pl.when(t == 0)
    def _():
        tail_scr[...] = jnp.zeros_like(tail_scr)
        hc_scr[...] = jnp.zeros_like(hc_scr)

    sub = lax.broadcasted_iota(jnp.int32, (SUBLANES, REC_BLOCK), 0)
    seg_rows = lambda s: slice(s * SEG_PITCH, s * SEG_PITCH + SEG)
    seg_step = lambda g: pl.ds(g, SUBLANES, stride=SEG_PITCH)
    step_rows = lambda g: slice(g * SUBLANES, (g + 1) * SUBLANES)

    for n in range(N_REC_BLOCKS):
        xf = xr_ref[:, n * REC_BLOCK:(n + 1) * REC_BLOCK].astype(F32)
        for s in range(SUBLANES):
            x_scr[n][seg_rows(s), :] = xf[s * SEG:(s + 1) * SEG]

    for n in range(N_REC_BLOCKS):
        cs = slice(n * REC_BLOCK, (n + 1) * REC_BLOCK)
        xs = [x_scr[n][seg_step(g), :] for g in range(SEG)]
        before = {}
        for j in range(1, CONV_WIDTH):
            cur = pltpu.roll(xs[SEG - j], 1, axis=0)
            before[-j] = jnp.where(sub == 0, tail_scr[j - 1, :, cs], cur)
            tail_scr[j - 1, :, cs] = cur
        x_at = lambda g: xs[g] if g >= 0 else before[g]
        taps = [cw_ref[CONV_WIDTH - 1 - j:CONV_WIDTH - j, cs] for j in range(CONV_WIDTH)]
        blocks = []
        for g in range(SEG):
            yb = cb_ref[:, cs]
            for j in range(CONV_WIDTH):
                yb = yb + taps[j] * x_at(g - j)
            blocks.append(yb)
        y = jnp.concatenate(blocks, axis=0)

        gates = jnp.dot(y.astype(BF16), wg_ref[n], preferred_element_type=F32)
        r = _sigmoid(gates[:, :REC_BLOCK] + ba_ref[:, cs])
        ig = _sigmoid(gates[:, REC_BLOCK:] + bx_ref[:, cs])
        log_a = r * (-LRU_C * jax.nn.softplus(-lam_ref[:, cs]))
        a = jnp.exp(log_a)
        gap = -jnp.tanh(log_a) * (a * a + 1.0)
        mult = jnp.where(gap > 0.0, gap * lax.rsqrt(gap), 0.0)
        a_scr[n][...] = a
        u_scr[n][...] = mult * (ig * y)

    for n in range(N_REC_BLOCKS):
        cs = slice(n * REC_BLOCK, (n + 1) * REC_BLOCK)
        p = jnp.ones((SUBLANES, REC_BLOCK), F32)
        q = jnp.zeros((SUBLANES, REC_BLOCK), F32)
        for g in range(SEG):
            ag = a_scr[n][step_rows(g), :]
            q = ag * q + u_scr[n][step_rows(g), :]
            p = ag * p
        p, q = _sublane_scan(p, q, sub)
        carry = hc_scr[:, cs]
        seg_end = q + p * carry
        h = jnp.where(sub == 0, carry, pltpu.roll(seg_end, 1, axis=0))
        hc_scr[:, cs] = jnp.broadcast_to(seg_end[SUBLANES - 1:SUBLANES], seg_end.shape)
        for g in range(SEG):
            h = a_scr[n][step_rows(g), :] * h + u_scr[n][step_rows(g), :]
            h_scr[n][seg_step(g), :] = h

    for n in range(N_REC_BLOCKS):
        cs = slice(n * REC_BLOCK, (n + 1) * REC_BLOCK)
        h = jnp.concatenate([h_scr[n][seg_rows(s), :] for s in range(SUBLANES)], axis=0)
        gr = gr_ref[:, cs].astype(F32)
        o_ref[:, cs] = (h * (gr * _sigmoid(gr))).astype(BF16)


def _rglru(proj, conv_w, conv_b, w_gate, b_a, b_x, lru_lambda, bsz, seq):
    n_t = seq // TS
    xr_group, gr_group = 4, 5
    full = lambda shape: pl.BlockSpec(shape, lambda b, t: (0,) * len(shape))
    slab = pltpu.VMEM((SUBLANES * SEG_PITCH, REC_BLOCK), F32)
    plain = pltpu.VMEM((TS, REC_BLOCK), F32)
    return pl.pallas_call(
        _rglru_kernel,
        grid=(bsz, n_t),
        in_specs=[
            pl.BlockSpec((TS, REC_WIDTH), lambda b, t: (b * n_t + t, xr_group)),
            pl.BlockSpec((TS, REC_WIDTH), lambda b, t: (b * n_t + t, gr_group)),
            full((CONV_WIDTH, REC_WIDTH)),
            full((1, REC_WIDTH)),
            full((N_REC_BLOCKS, REC_BLOCK, 2 * REC_BLOCK)),
            full((1, REC_WIDTH)), full((1, REC_WIDTH)), full((1, REC_WIDTH)),
        ],
        out_specs=pl.BlockSpec((TS, REC_WIDTH), lambda b, t: (b * n_t + t, 0)),
        out_shape=jax.ShapeDtypeStruct((bsz * seq, REC_WIDTH), BF16),
        scratch_shapes=[
            pltpu.VMEM((CONV_WIDTH - 1, SUBLANES, REC_WIDTH), F32),
            pltpu.VMEM((SUBLANES, REC_WIDTH), F32),
        ] + [slab] * (2 * N_REC_BLOCKS) + [plain] * (2 * N_REC_BLOCKS),
        compiler_params=pltpu.CompilerParams(
            dimension_semantics=("arbitrary", "arbitrary"),
            vmem_limit_bytes=VMEM_LIMIT),
        name="conv_rglru",
    )(proj, proj, conv_w, conv_b, w_gate, b_a, b_x, lru_lambda)


def _outproj_kernel(x_ref, o_ref, r_ref, w_ref, gain_ref, y_ref, w_scr):
    @pl.when(pl.program_id(0) == 0)
    def _():
        w_scr[...] = w_ref[...].astype(BF16)

    y = x_ref[...]
    y = y + jnp.dot(o_ref[...], w_scr[0:ATTN_WIDTH, :], preferred_element_type=F32)
    y = y + jnp.dot(r_ref[...], w_scr[ATTN_WIDTH:, :], preferred_element_type=F32)
    ms = jnp.mean(y * y, axis=-1, keepdims=True)
    y_ref[...] = y * lax.rsqrt(ms + NORM_EPS) * gain_ref[...]


def _outproj(x2, o, r, w_out, final_gain):
    m = x2.shape[0]
    return pl.pallas_call(
        _outproj_kernel,
        grid=(m // TM_OUT,),
        in_specs=[
            pl.BlockSpec((TM_OUT, D_MODEL), lambda i: (i, 0)),
            pl.BlockSpec((TM_OUT, ATTN_WIDTH), lambda i: (i, 0)),
            pl.BlockSpec((TM_OUT, REC_WIDTH), lambda i: (i, 0)),
            pl.BlockSpec((ATTN_WIDTH + REC_WIDTH, D_MODEL), lambda i: (0, 0)),
            pl.BlockSpec((1, D_MODEL), lambda i: (0, 0)),
        ],
        out_specs=pl.BlockSpec((TM_OUT, D_MODEL), lambda i: (i, 0)),
        out_shape=jax.ShapeDtypeStruct((m, D_MODEL), F32),
        scratch_shapes=[pltpu.VMEM((ATTN_WIDTH + REC_WIDTH, D_MODEL), BF16)],
        compiler_params=pltpu.CompilerParams(
            dimension_semantics=("arbitrary",),
            vmem_limit_bytes=VMEM_LIMIT),
        name="outproj_norm",
    )(x2, o, r, w_out, final_gain)


def kernel(x, positions, norm_gain, w_in, lambda_q1, lambda_k1, lambda_q2, lambda_k2,
           subln_gain, conv_w, conv_b, w_a, b_a, w_x, b_x, lru_lambda, w_out, final_gain):
    bsz, seq, _ = x.shape
    assert x.shape[2] == D_MODEL and norm_gain.shape[0] == 1
    assert seq % max(TQ, TS) == 0 and (bsz * seq) % max(TM_IN, TM_OUT) == 0
    x2 = x.reshape(bsz * seq, D_MODEL)
    pos2 = positions.reshape(bsz * seq, 1)
    inv_freq = ROPE_THETA ** (-jnp.arange(0, QK_DIM, 2, dtype=F32) / QK_DIM)
    invf = jnp.tile(inv_freq, LANES // ROT_HALF).reshape(1, LANES)

    proj = _inproj(x2, pos2, invf, norm_gain[0].reshape(1, D_MODEL), w_in[0])
    o = _attention(proj, lambda_q1[0].reshape(1, QK_DIM), lambda_k1[0].reshape(1, QK_DIM),
                   lambda_q2[0].reshape(1, QK_DIM), lambda_k2[0].reshape(1, QK_DIM),
                   subln_gain[0].reshape(1, HEAD_DIM), bsz, seq)
    w_gate = jnp.concatenate([w_a[0], w_x[0]], axis=-1).astype(BF16)
    r = _rglru(proj, conv_w[0], conv_b[0].reshape(1, REC_WIDTH), w_gate,
               b_a[0].reshape(1, REC_WIDTH), b_x[0].reshape(1, REC_WIDTH),
               lru_lambda[0].reshape(1, REC_WIDTH), bsz, seq)
    y = _outproj(x2, o, r, w_out[0], final_gain.reshape(1, D_MODEL))
    return y.reshape(bsz, seq, D_MODEL)
```

```python
import math

import jax
import jax.numpy as jnp
from jax import lax
from jax.experimental import pallas as pl
from jax.experimental.pallas import tpu as pltpu

F32 = jnp.float32
BF16 = jnp.bfloat16

D_MODEL = 2048
ATTN_WIDTH = D_MODEL // 2
REC_WIDTH = D_MODEL // 2
N_ATTN_HEADS = 8
HEAD_DIM = ATTN_WIDTH // N_ATTN_HEADS
QK_DIM = HEAD_DIM // 2
ROT_HALF = QK_DIM // 2
N_REC_BLOCKS = 8
REC_BLOCK = REC_WIDTH // N_REC_BLOCKS
CONV_WIDTH = 4
LRU_C = 8.0
ROPE_THETA = 10000.0
NORM_EPS = 1e-6
IN_WIDTH = 4 * ATTN_WIDTH + 2 * REC_WIDTH
LAM_INIT = 0.8 - 0.6 * math.exp(-0.3 * 0)
QK_SCALE = QK_DIM ** -0.5
LOG2E = math.log2(math.e)

SUBLANES = 8
LANES = 128

TM_IN = 512
COL_GROUP = 256
TQ = 256
SOFTMAX_ROWS = 64
HEADS_PER_STEP = 2
TS = 256
TM_OUT = 512
VMEM_LIMIT = 48 * 1024 * 1024
VMEM_LIMIT_INPROJ = 58 * 1024 * 1024


def _sigmoid(x):
    return 0.5 * jnp.tanh(0.5 * x) + 0.5


N_COL_GROUPS = IN_WIDTH // COL_GROUP
N_ROPE_GROUPS = 2 * ATTN_WIDTH // COL_GROUP
N_Q_GROUPS = ATTN_WIDTH // COL_GROUP
GROUP_ORDER = tuple(range(N_ROPE_GROUPS, N_COL_GROUPS)) + tuple(range(N_ROPE_GROUPS))
STAGE_GROUPS = 2
N_STAGE_STEPS = N_COL_GROUPS // STAGE_GROUPS
STAGE_WIDTH = STAGE_GROUPS * COL_GROUP


def _stage_block(t):
    return GROUP_ORDER[t * STAGE_GROUPS] // STAGE_GROUPS


def _rope_partner(blk, lane):
    fwd = pltpu.roll(blk, LANES - ROT_HALF, axis=1)
    bwd = pltpu.roll(blk, ROT_HALF, axis=1)
    return jnp.where((lane % QK_DIM) < ROT_HALF, fwd, bwd)


def _inproj_kernel(x_ref, pos_ref, invf_ref, gain_ref, w_ref, o_ref,
                   w_scr, h_scr, rstd_scr, cos_scr, sin_scr):
    tm = x_ref.shape[0]
    q_scale = QK_SCALE * LOG2E

    def row_prologue():
        x = x_ref[...]
        h_scr[...] = (x * gain_ref[...]).astype(BF16)
        rstd = lax.rsqrt(jnp.mean(x * x, axis=-1, keepdims=True) + NORM_EPS)
        rstd_scr[...] = jnp.broadcast_to(rstd, (tm, LANES))
        ang = pos_ref[...].astype(F32) * invf_ref[...]
        lane = lax.broadcasted_iota(jnp.int32, (tm, LANES), 1)
        sign = jnp.where((lane % QK_DIM) < ROT_HALF, -1.0, 1.0).astype(F32)
        cos_scr[...] = jnp.cos(ang)
        sin_scr[...] = jnp.sin(ang) * sign

    def project(group):
        cs = slice(group * COL_GROUP, (group + 1) * COL_GROUP)
        acc = jnp.dot(h_scr[...], w_scr[:, cs], preferred_element_type=F32)
        rstd = rstd_scr[...]
        if group >= N_ROPE_GROUPS:
            o_ref[:, cs] = (acc * jnp.concatenate([rstd, rstd], axis=1)).astype(BF16)
            return
        cosf, sinf = cos_scr[...], sin_scr[...]
        if group < N_Q_GROUPS:
            cosf, sinf = cosf * q_scale, sinf * q_scale
        lane = lax.broadcasted_iota(jnp.int32, (tm, LANES), 1)
        for half in range(COL_GROUP // LANES):
            blk = acc[:, half * LANES:(half + 1) * LANES] * rstd
            rot = blk * cosf + _rope_partner(blk, lane) * sinf
            c0 = group * COL_GROUP + half * LANES
            o_ref[:, c0:c0 + LANES] = rot.astype(BF16)

    step = pl.program_id(0)

    for t in range(N_STAGE_STEPS):
        @pl.when(step == t)
        def _(t=t):
            c0 = _stage_block(t) * STAGE_WIDTH
            w_scr[:, c0:c0 + STAGE_WIDTH] = w_ref[...].astype(BF16)
            if t == 0:
                row_prologue()
            for group in GROUP_ORDER[t * STAGE_GROUPS:(t + 1) * STAGE_GROUPS]:
                project(group)

    @pl.when(step >= N_STAGE_STEPS)
    def _():
        row_prologue()
        for group in GROUP_ORDER:
            project(group)


def _inproj(x2, pos2, invf, gain, w_in):
    m = x2.shape[0]
    n_row_tiles = m // TM_IN
    row_tile = lambda s: jnp.maximum(s - (N_STAGE_STEPS - 1), 0)
    last_block = _stage_block(N_STAGE_STEPS - 1)
    stage_block = lambda s: jnp.where(
        s < N_STAGE_STEPS, (s + _stage_block(0)) % N_STAGE_STEPS, last_block)
    assert all(_stage_block(t) == (t + _stage_block(0)) % N_STAGE_STEPS
               for t in range(N_STAGE_STEPS))
    return pl.pallas_call(
        _inproj_kernel,
        grid=(n_row_tiles + N_STAGE_STEPS - 1,),
        in_specs=[
            pl.BlockSpec((TM_IN, D_MODEL), lambda s: (row_tile(s), 0)),
            pl.BlockSpec((TM_IN, 1), lambda s: (row_tile(s), 0)),
            pl.BlockSpec((1, LANES), lambda s: (0, 0)),
            pl.BlockSpec((1, D_MODEL), lambda s: (0, 0)),
            pl.BlockSpec((D_MODEL, STAGE_WIDTH), lambda s: (0, stage_block(s))),
        ],
        out_specs=pl.BlockSpec((TM_IN, IN_WIDTH), lambda s: (row_tile(s), 0)),
        out_shape=jax.ShapeDtypeStruct((m, IN_WIDTH), BF16),
        scratch_shapes=[
            pltpu.VMEM((D_MODEL, IN_WIDTH), BF16),
            pltpu.VMEM((TM_IN, D_MODEL), BF16),
            pltpu.VMEM((TM_IN, LANES), F32),
            pltpu.VMEM((TM_IN, LANES), F32),
            pltpu.VMEM((TM_IN, LANES), F32),
        ],
        compiler_params=pltpu.CompilerParams(
            dimension_semantics=("arbitrary",),
            vmem_limit_bytes=VMEM_LIMIT_INPROJ),
        name="inproj_rope",
    )(x2, pos2, invf, gain, w_in)


def _dot_nt(a, b):
    return lax.dot_general(a, b, (((1,), (1,)), ((), ())), preferred_element_type=F32)


def _attn_kernel(q_ref, k_ref, v_ref, g_ref, lq1_ref, lk1_ref, lq2_ref, lk2_ref,
                 sg_ref, o_ref, s_scr, p_scr):
    seq = q_ref.shape[0]
    lam = (jnp.exp(jnp.sum(lq1_ref[...] * lk1_ref[...], axis=-1, keepdims=True))
           - jnp.exp(jnp.sum(lq2_ref[...] * lk2_ref[...], axis=-1, keepdims=True))
           + LAM_INIT)
    lane = lax.broadcasted_iota(jnp.int32, (TQ, LANES), 1)
    row = lax.broadcasted_iota(jnp.int32, (2 * TQ, TQ), 0)
    col = lax.broadcasted_iota(jnp.int32, (2 * TQ, TQ), 1)
    causal = col <= jnp.where(row >= TQ, row - TQ, row)

    n_q = seq // TQ
    for tile in range(HEADS_PER_STEP * n_q):
        head, i = divmod(tile, n_q)
        hl = slice(head * HEAD_DIM, (head + 1) * HEAD_DIM)
        buf = tile % 2
        w0, w = i * TQ, (i + 1) * TQ
        q = q_ref[w0:w, hl]
        zero = jnp.zeros_like(q)
        qcat = jnp.concatenate([jnp.where(lane < QK_DIM, q, zero),
                                jnp.where(lane < QK_DIM, zero, q)], axis=0)
        if i > 0:
            s_scr[buf, :, 0:w0] = _dot_nt(qcat, k_ref[0:w0, hl])
        s_scr[buf, :, w0:w] = jnp.where(causal, _dot_nt(qcat, k_ref[w0:w, hl]), -jnp.inf)

        for rb in range(2 * TQ // SOFTMAX_ROWS):
            rows = slice(rb * SOFTMAX_ROWS, (rb + 1) * SOFTMAX_ROWS)
            m = jnp.max(s_scr[buf, rows, 0:w], axis=1, keepdims=True)
            p_scr[buf, rows, 0:w] = jnp.exp2(s_scr[buf, rows, 0:w] - m).astype(BF16)

        v_ext = jnp.concatenate([v_ref[0:w, hl], jnp.ones((w, HEAD_DIM), BF16)], axis=1)
        acc1 = jnp.dot(p_scr[buf, 0:TQ, 0:w], v_ext, preferred_element_type=F32)
        acc2 = jnp.dot(p_scr[buf, TQ:, 0:w], v_ext, preferred_element_type=F32)
        o = (acc1[:, :HEAD_DIM] * (1.0 / acc1[:, HEAD_DIM:])
             - acc2[:, :HEAD_DIM] * (lam / acc2[:, HEAD_DIM:]))
        ms = jnp.mean(o * o, axis=-1, keepdims=True)
        o = o * lax.rsqrt(ms + NORM_EPS) * sg_ref[...] * (1.0 - LAM_INIT)
        g = g_ref[w0:w, hl].astype(F32)
        o_ref[w0:w, hl] = (o * (g * _sigmoid(g))).astype(BF16)


def _attention(proj, lq1, lk1, lq2, lk2, subln, bsz, seq):
    head_block = (seq, HEADS_PER_STEP * HEAD_DIM)
    n_steps = N_ATTN_HEADS // HEADS_PER_STEP
    q_group, k_group, v_group, g_group = (g * n_steps for g in range(4))
    small = lambda w: pl.BlockSpec((1, w), lambda b, h: (0, 0))
    return pl.pallas_call(
        _attn_kernel,
        grid=(bsz, n_steps),
        in_specs=[
            pl.BlockSpec(head_block, lambda b, h: (b, q_group + h)),
            pl.BlockSpec(head_block, lambda b, h: (b, k_group + h)),
            pl.BlockSpec(head_block, lambda b, h: (b, v_group + h)),
            pl.BlockSpec(head_block, lambda b, h: (b, g_group + h)),
            small(QK_DIM), small(QK_DIM), small(QK_DIM), small(QK_DIM),
            small(HEAD_DIM),
        ],
        out_specs=pl.BlockSpec(head_block, lambda b, h: (b, h)),
        out_shape=jax.ShapeDtypeStruct((bsz * seq, ATTN_WIDTH), BF16),
        scratch_shapes=[pltpu.VMEM((2, 2 * TQ, seq), F32),
                        pltpu.VMEM((2, 2 * TQ, seq), BF16)],
        compiler_params=pltpu.CompilerParams(
            dimension_semantics=("arbitrary", "arbitrary"),
            vmem_limit_bytes=VMEM_LIMIT),
        name="diff_attention",
    )(proj, proj, proj, proj, lq1, lk1, lq2, lk2, subln)


SEG = TS // SUBLANES
SEG_PITCH = SEG + SUBLANES


def _sublane_scan(p, q, sub):
    d = 1
    while d < SUBLANES:
        keep = sub >= d
        p_prev = pltpu.roll(p, d, axis=0)
        q_prev = pltpu.roll(q, d, axis=0)
        q = jnp.where(keep, p * q_prev, 0.0) + q
        p = jnp.where(keep, p * p_prev, p)
        d *= 2
    return p, q


def _rglru_kernel(xr_ref, gr_ref, cw_ref, cb_ref, wg_ref, ba_ref, bx_ref, lam_ref,
                  o_ref, tail_scr, hc_scr, *scratch):
    x_scr, h_scr, a_scr, u_scr = (scratch[k * N_REC_BLOCKS:(k + 1) * N_REC_BLOCKS]
                                  for k in range(4))
    t = pl.program_id(1)

    @pl.when(t == 0)
    def _():
        tail_scr[...] = jnp.zeros_like(tail_scr)
        hc_scr[...] = jnp.zeros_like(hc_scr)

    sub = lax.broadcasted_iota(jnp.int32, (SUBLANES, REC_BLOCK), 0)
    seg_rows = lambda s: slice(s * SEG_PITCH, s * SEG_PITCH + SEG)
    seg_step = lambda g: pl.ds(g, SUBLANES, stride=SEG_PITCH)
    step_rows = lambda g: slice(g * SUBLANES, (g + 1) * SUBLANES)

    for n in range(N_REC_BLOCKS):
        xf = xr_ref[:, n * REC_BLOCK:(n + 1) * REC_BLOCK].astype(F32)
        for s in range(SUBLANES):
            x_scr[n][seg_rows(s), :] = xf[s * SEG:(s + 1) * SEG]

    for n in range(N_REC_BLOCKS):
        cs = slice(n * REC_BLOCK, (n + 1) * REC_BLOCK)
        xs = [x_scr[n][seg_step(g), :] for g in range(SEG)]
        before = {}
        for j in range(1, CONV_WIDTH):
            cur = pltpu.roll(xs[SEG - j], 1, axis=0)
            before[-j] = jnp.where(sub == 0, tail_scr[j - 1, :, cs], cur)
            tail_scr[j - 1, :, cs] = cur
        x_at = lambda g: xs[g] if g >= 0 else before[g]
        taps = [cw_ref[CONV_WIDTH - 1 - j:CONV_WIDTH - j, cs] for j in range(CONV_WIDTH)]
        blocks = []
        for g in range(SEG):
            yb = cb_ref[:, cs]
            for j in range(CONV_WIDTH):
                yb = yb + taps[j] * x_at(g - j)
            blocks.append(yb)
        y = jnp.concatenate(blocks, axis=0)

        gates = jnp.dot(y.astype(BF16), wg_ref[n], preferred_element_type=F32)
        r = _sigmoid(gates[:, :REC_BLOCK] + ba_ref[:, cs])
        ig = _sigmoid(gates[:, REC_BLOCK:] + bx_ref[:, cs])
        log_a = r * (-LRU_C * jax.nn.softplus(-lam_ref[:, cs]))
        a = jnp.exp(log_a)
        gap = -jnp.tanh(log_a) * (a * a + 1.0)
        mult = jnp.where(gap > 0.0, gap * lax.rsqrt(gap), 0.0)
        a_scr[n][...] = a
        u_scr[n][...] = mult * (ig * y)

    for n in range(N_REC_BLOCKS):
        cs = slice(n * REC_BLOCK, (n + 1) * REC_BLOCK)
        p = jnp.ones((SUBLANES, REC_BLOCK), F32)
        q = jnp.zeros((SUBLANES, REC_BLOCK), F32)
        for g in range(SEG):
            ag = a_scr[n][step_rows(g), :]
            q = ag * q + u_scr[n][step_rows(g), :]
            p = ag * p
        p, q = _sublane_scan(p, q, sub)
        carry = hc_scr[:, cs]
        seg_end = q + p * carry
        h = jnp.where(sub == 0, carry, pltpu.roll(seg_end, 1, axis=0))
        hc_scr[:, cs] = jnp.broadcast_to(seg_end[SUBLANES - 1:SUBLANES], seg_end.shape)
        for g in range(SEG):
            h = a_scr[n][step_rows(g), :] * h + u_scr[n][step_rows(g), :]
            h_scr[n][seg_step(g), :] = h

    for n in range(N_REC_BLOCKS):
        cs = slice(n * REC_BLOCK, (n + 1) * REC_BLOCK)
        h = jnp.concatenate([h_scr[n][seg_rows(s), :] for s in range(SUBLANES)], axis=0)
        gr = gr_ref[:, cs].astype(F32)
        o_ref[:, cs] = (h * (gr * _sigmoid(gr))).astype(BF16)


def _rglru(proj, conv_w, conv_b, w_gate, b_a, b_x, lru_lambda, bsz, seq):
    n_t = seq // TS
    xr_group, gr_group = 4, 5
    full = lambda shape: pl.BlockSpec(shape, lambda b, t: (0,) * len(shape))
    slab = pltpu.VMEM((SUBLANES * SEG_PITCH, REC_BLOCK), F32)
    plain = pltpu.VMEM((TS, REC_BLOCK), F32)
    return pl.pallas_call(
        _rglru_kernel,
        grid=(bsz, n_t),
        in_specs=[
            pl.BlockSpec((TS, REC_WIDTH), lambda b, t: (b * n_t + t, xr_group)),
            pl.BlockSpec((TS, REC_WIDTH), lambda b, t: (b * n_t + t, gr_group)),
            full((CONV_WIDTH, REC_WIDTH)),
            full((1, REC_WIDTH)),
            full((N_REC_BLOCKS, REC_BLOCK, 2 * REC_BLOCK)),
            full((1, REC_WIDTH)), full((1, REC_WIDTH)), full((1, REC_WIDTH)),
        ],
        out_specs=pl.BlockSpec((TS, REC_WIDTH), lambda b, t: (b * n_t + t, 0)),
        out_shape=jax.ShapeDtypeStruct((bsz * seq, REC_WIDTH), BF16),
        scratch_shapes=[
            pltpu.VMEM((CONV_WIDTH - 1, SUBLANES, REC_WIDTH), F32),
            pltpu.VMEM((SUBLANES, REC_WIDTH), F32),
        ] + [slab] * (2 * N_REC_BLOCKS) + [plain] * (2 * N_REC_BLOCKS),
        compiler_params=pltpu.CompilerParams(
            dimension_semantics=("arbitrary", "arbitrary"),
            vmem_limit_bytes=VMEM_LIMIT),
        name="conv_rglru",
    )(proj, proj, conv_w, conv_b, w_gate, b_a, b_x, lru_lambda)


def _outproj_kernel(x_ref, o_ref, r_ref, w_ref, gain_ref, y_ref, w_scr):
    @pl.when(pl.program_id(0) == 0)
    def _():
        w_scr[...] = w_ref[...].astype(BF16)

    y = x_ref[...]
    y = y + jnp.dot(o_ref[...], w_scr[0:ATTN_WIDTH, :], preferred_element_type=F32)
    y = y + jnp.dot(r_ref[...], w_scr[ATTN_WIDTH:, :], preferred_element_type=F32)
    ms = jnp.mean(y * y, axis=-1, keepdims=True)
    y_ref[...] = y * lax.rsqrt(ms + NORM_EPS) * gain_ref[...]


def _outproj(x2, o, r, w_out, final_gain):
    m = x2.shape[0]
    return pl.pallas_call(
        _outproj_kernel,
        grid=(m // TM_OUT,),
        in_specs=[
            pl.BlockSpec((TM_OUT, D_MODEL), lambda i: (i, 0)),
            pl.BlockSpec((TM_OUT, ATTN_WIDTH), lambda i: (i, 0)),
            pl.BlockSpec((TM_OUT, REC_WIDTH), lambda i: (i, 0)),
            pl.BlockSpec((ATTN_WIDTH + REC_WIDTH, D_MODEL), lambda i: (0, 0)),
            pl.BlockSpec((1, D_MODEL), lambda i: (0, 0)),
        ],
        out_specs=pl.BlockSpec((TM_OUT, D_MODEL), lambda i: (i, 0)),
        out_shape=jax.ShapeDtypeStruct((m, D_MODEL), F32),
        scratch_shapes=[pltpu.VMEM((ATTN_WIDTH + REC_WIDTH, D_MODEL), BF16)],
        compiler_params=pltpu.CompilerParams(
            dimension_semantics=("arbitrary",),
            vmem_limit_bytes=VMEM_LIMIT),
        name="outproj_norm",
    )(x2, o, r, w_out, final_gain)


def kernel(x, positions, norm_gain, w_in, lambda_q1, lambda_k1, lambda_q2, lambda_k2,
           subln_gain, conv_w, conv_b, w_a, b_a, w_x, b_x, lru_lambda, w_out, final_gain):
    bsz, seq, _ = x.shape
    assert x.shape[2] == D_MODEL and norm_gain.shape[0] == 1
    assert seq % max(TQ, TS) == 0 and (bsz * seq) % max(TM_IN, TM_OUT) == 0
    x2 = x.reshape(bsz * seq, D_MODEL)
    pos2 = positions.reshape(bsz * seq, 1)
    inv_freq = ROPE_THETA ** (-jnp.arange(0, QK_DIM, 2, dtype=F32) / QK_DIM)
    invf = jnp.tile(inv_freq, LANES // ROT_HALF).reshape(1, LANES)

    proj = _inproj(x2, pos2, invf, norm_gain[0].reshape(1, D_MODEL), w_in[0])
    o = _attention(proj, lambda_q1[0].reshape(1, QK_DIM), lambda_k1[0].reshape(1, QK_DIM),
                   lambda_q2[0].reshape(1, QK_DIM), lambda_k2[0].reshape(1, QK_DIM),
                   subln_gain[0].reshape(1, HEAD_DIM), bsz, seq)
    w_gate = jnp.concatenate([w_a[0], w_x[0]], axis=-1).astype(BF16)
    r = _rglru(proj, conv_w[0], conv_b[0].reshape(1, REC_WIDTH), w_gate,
               b_a[0].reshape(1, REC_WIDTH), b_x[0].reshape(1, REC_WIDTH),
               lru_lambda[0].reshape(1, REC_WIDTH), bsz, seq)
    y = _outproj(x2, o, r, w_out[0], final_gain.reshape(1, D_MODEL))
    return y.reshape(bsz, seq, D_MODEL)
```
